```python
import jax, jax.numpy as jnp
from jax import lax
import numpy as np

D_MODEL = 1024
BATCH = 2
SEQ = 8192
DEPTH = 1

N_HEADS = 8
HEAD_DIM = 128
D_ATTN = N_HEADS * HEAD_DIM
N_IDX_HEADS = 16
IDX_DIM = 64
TOPK_MAX = 256
Q_BLOCK = 128
NEG_INF = -1e30
D_RNN = 1024
N_RNN_BLOCKS = 8
RNN_BLOCK = D_RNN // N_RNN_BLOCKS
CONV_WIDTH = 4
LRU_C = 8.0
D_FF = 2816
LN_EPS = 1e-5
DEEPNORM_ALPHA = (2.0 * DEPTH) ** 0.25
DEEPNORM_BETA = (8.0 * DEPTH) ** -0.25
SPLIT_SIZES = (D_ATTN, D_ATTN, D_ATTN, N_IDX_HEADS * IDX_DIM, IDX_DIM, N_IDX_HEADS, D_RNN, D_RNN, D_MODEL, D_MODEL)
D_IN = D_ATTN * 3 + N_IDX_HEADS * IDX_DIM + IDX_DIM + N_IDX_HEADS + D_RNN * 2 + D_MODEL * 2

kernel_name = "hybrid_dsa_rglru_macaron_deepnorm"


def split_offsets():
    offs, acc = [], 0
    for s in SPLIT_SIZES[:-1]:
        acc += s
        offs.append(acc)
    return offs


def layer_norm(x, g, b):
    xf = x.astype(jnp.float32)
    mu = jnp.mean(xf, axis=-1, keepdims=True)
    var = jnp.mean(jnp.square(xf - mu), axis=-1, keepdims=True)
    y = (xf - mu) * lax.rsqrt(var + LN_EPS)
    return (y * g.astype(jnp.float32) + b.astype(jnp.float32)).astype(x.dtype)


def swiglu(x, w_gate, w_up, w_down):
    return (jax.nn.silu(x @ w_gate) * (x @ w_up)) @ w_down


def alibi_slopes(n):
    return jnp.power(2.0, -8.0 * jnp.arange(1, n + 1, dtype=jnp.float32) / n)


def dsa_attention(q, k, v, q_idx, k_idx, w_idx):
    B, S = q.shape[0], q.shape[1]
    topk = min(TOPK_MAX, S // 4)
    n_blocks = S // Q_BLOCK
    slopes = alibi_slopes(N_HEADS)
    key_pos = jnp.arange(S, dtype=jnp.int32)
    scale = HEAD_DIM ** -0.5

    def to_blocks(t):
        return jnp.moveaxis(t.reshape((B, n_blocks, Q_BLOCK) + t.shape[2:]), 1, 0)

    def attend_block(args):
        qb, qib, wb, start = args
        q_pos = start + jnp.arange(Q_BLOCK, dtype=jnp.int32)
        rel = jax.nn.relu(jnp.einsum('bqhd,bsd->bqhs', qib, k_idx))
        score = jnp.einsum('bqh,bqhs->bqs', wb, rel).astype(jnp.float32)
        causal = key_pos[None, :] <= q_pos[:, None]
        score = jnp.where(causal[None], score, NEG_INF)
        _, sel = lax.top_k(score, topk)
        k_sel = jax.vmap(lambda kk, ii: kk[ii])(k, sel)
        v_sel = jax.vmap(lambda vv, ii: vv[ii])(v, sel)
        logits = jnp.einsum('bqhd,bqkhd->bhqk', qb, k_sel).astype(jnp.float32) * scale
        dist = (q_pos[None, :, None] - sel).astype(jnp.float32)
        logits = logits - slopes[None, :, None, None] * dist[:, None]
        valid = sel <= q_pos[None, :, None]
        logits = jnp.where(valid[:, None], logits, NEG_INF)
        p = jax.nn.softmax(logits, axis=-1)
        return jnp.einsum('bhqk,bqkhd->bqhd', p.astype(v.dtype), v_sel)

    starts = jnp.arange(n_blocks, dtype=jnp.int32) * Q_BLOCK
    out = lax.map(attend_block, (to_blocks(q), to_blocks(q_idx), to_blocks(w_idx), starts))
    return jnp.moveaxis(out, 0, 1).reshape(B, S, N_HEADS * HEAD_DIM)


def causal_depthwise_conv(x, w, b):
    C = x.shape[-1]
    y = lax.conv_general_dilated(x, w[:, None, :], window_strides=(1,), padding=[(CONV_WIDTH - 1, 0)],
                                 dimension_numbers=('NWC', 'WIO', 'NWC'), feature_group_count=C)
    return y + b


def rg_lru(x, w_a, b_a, w_x, b_x, lam):
    B, S, _ = x.shape
    xf = x.astype(jnp.float32)
    xb = xf.reshape(B, S, N_RNN_BLOCKS, RNN_BLOCK)
    r = jax.nn.sigmoid(jnp.einsum('bsnc,ncd->bsnd', xb, w_a.astype(jnp.float32)).reshape(B, S, D_RNN) + b_a.astype(jnp.float32))
    i = jax.nn.sigmoid(jnp.einsum('bsnc,ncd->bsnd', xb, w_x.astype(jnp.float32)).reshape(B, S, D_RNN) + b_x.astype(jnp.float32))
    log_a = -LRU_C * r * jax.nn.softplus(-lam.astype(jnp.float32))
    a = jnp.exp(log_a)
    bx = jnp.sqrt(-jnp.expm1(2.0 * log_a)) * (i * xf)

    def combine(c1, c2):
        a1, b1 = c1
        a2, b2 = c2
        return a1 * a2, a2 * b1 + b2

    _, h = lax.associative_scan(combine, (a, bx), axis=1)
    return h.astype(x.dtype)


def hybrid_mixer(u, w_in, b_in, conv_w, conv_b, lru_w_a, lru_b_a, lru_w_x, lru_b_x, lru_lambda,
                 w_proj_attn, w_proj_rnn, w_out):
    B, S, _ = u.shape
    z = u @ w_in + b_in
    q, k, v, qi, ki, wi, xr, gr, ga, gb = jnp.split(z, split_offsets(), axis=-1)
    q = q.reshape(B, S, N_HEADS, HEAD_DIM)
    k = k.reshape(B, S, N_HEADS, HEAD_DIM)
    v = v.reshape(B, S, N_HEADS, HEAD_DIM)
    qi = qi.reshape(B, S, N_IDX_HEADS, IDX_DIM)
    wi = wi * (N_IDX_HEADS ** -0.5 * IDX_DIM ** -0.5)
    o_attn = dsa_attention(q, k, v, qi, ki, wi)
    hr = rg_lru(causal_depthwise_conv(xr, conv_w, conv_b), lru_w_a, lru_b_a, lru_w_x, lru_b_x, lru_lambda)
    o_rnn = hr * jax.nn.gelu(gr)
    y = jax.nn.sigmoid(ga) * (o_attn @ w_proj_attn) + jax.nn.sigmoid(gb) * (o_rnn @ w_proj_rnn)
    return y @ w_out


def setup_inputs(seed: int = 0) -> dict:
    key = jax.random.key(seed)
    ks = jax.random.split(key, 32)
    L = DEPTH

    def normal(k, shape, scale):
        return jax.random.normal(k, shape, jnp.float32) * scale

    def gain(k):
        return 1.0 + normal(k, (L, D_MODEL), 0.01)

    a0 = jax.random.uniform(ks[20], (L, D_RNN), jnp.float32, 0.9, 0.999)
    s0 = jnp.power(a0, 1.0 / LRU_C)
    return {
        'x': normal(ks[0], (BATCH, SEQ, D_MODEL), 1.0),
        'ln1_g': gain(ks[1]),
        'ln1_b': normal(ks[2], (L, D_MODEL), 0.01),
        'ffn1_w_gate': normal(ks[3], (L, D_MODEL, D_FF), D_MODEL ** -0.5),
        'ffn1_w_up': normal(ks[4], (L, D_MODEL, D_FF), D_MODEL ** -0.5),
        'ffn1_w_down': normal(ks[5], (L, D_FF, D_MODEL), DEEPNORM_BETA * D_FF ** -0.5),
        'w_in': normal(ks[6], (L, D_MODEL, D_IN), D_MODEL ** -0.5),
        'b_in': normal(ks[7], (L, D_IN), 0.01),
        'conv_w': normal(ks[8], (L, CONV_WIDTH, D_RNN), CONV_WIDTH ** -0.5),
        'conv_b': normal(ks[9], (L, D_RNN), 0.01),
        'lru_w_a': normal(ks[10], (L, N_RNN_BLOCKS, RNN_BLOCK, RNN_BLOCK), RNN_BLOCK ** -0.5),
        'lru_b_a': normal(ks[11], (L, D_RNN), 0.01),
        'lru_w_x': normal(ks[12], (L, N_RNN_BLOCKS, RNN_BLOCK, RNN_BLOCK), RNN_BLOCK ** -0.5),
        'lru_b_x': normal(ks[13], (L, D_RNN), 0.01),
        'lru_lambda': jnp.log(s0) - jnp.log1p(-s0),
        'w_proj_attn': normal(ks[14], (L, D_ATTN, D_MODEL), D_ATTN ** -0.5),
        'w_proj_rnn': normal(ks[15], (L, D_RNN, D_MODEL), D_RNN ** -0.5),
        'w_out': normal(ks[16], (L, D_MODEL, D_MODEL), DEEPNORM_BETA * D_MODEL ** -0.5),
        'ln2_g': gain(ks[17]),
        'ln2_b': normal(ks[18], (L, D_MODEL), 0.01),
        'ffn2_w_gate': normal(ks[21], (L, D_MODEL, D_FF), D_MODEL ** -0.5),
        'ffn2_w_up': normal(ks[22], (L, D_MODEL, D_FF), D_MODEL ** -0.5),
        'ffn2_w_down': normal(ks[23], (L, D_FF, D_MODEL), DEEPNORM_BETA * D_FF ** -0.5),
        'ln3_g': gain(ks[24]),
        'ln3_b': normal(ks[25], (L, D_MODEL), 0.01),
    }


def reference(x, ln1_g, ln1_b, ffn1_w_gate, ffn1_w_up, ffn1_w_down, w_in, b_in, conv_w, conv_b,
              lru_w_a, lru_b_a, lru_w_x, lru_b_x, lru_lambda, w_proj_attn, w_proj_rnn, w_out,
              ln2_g, ln2_b, ffn2_w_gate, ffn2_w_up, ffn2_w_down, ln3_g, ln3_b):
    h = x
    for l in range(DEPTH):
        h = layer_norm(DEEPNORM_ALPHA * h + 0.5 * swiglu(h, ffn1_w_gate[l], ffn1_w_up[l], ffn1_w_down[l]), ln1_g[l], ln1_b[l])
        mix = hybrid_mixer(h, w_in[l], b_in[l], conv_w[l], conv_b[l], lru_w_a[l], lru_b_a[l], lru_w_x[l], lru_b_x[l],
                           lru_lambda[l], w_proj_attn[l], w_proj_rnn[l], w_out[l])
        h = layer_norm(DEEPNORM_ALPHA * h + mix, ln2_g[l], ln2_b[l])
        h = layer_norm(DEEPNORM_ALPHA * h + 0.5 * swiglu(h, ffn2_w_gate[l], ffn2_w_up[l], ffn2_w_down[l]), ln3_g[l], ln3_b[l])
    return h
```

```python
import functools

import jax
import jax.numpy as jnp
import numpy as np
from jax import lax
from jax.experimental import pallas as pl
from jax.experimental.pallas import tpu as pltpu

F32 = jnp.float32
BF16 = jnp.bfloat16
I32 = jnp.int32

N_HEADS = 8
HEAD_DIM = 128
N_IDX_HEADS = 16
IDX_DIM = 64
TOPK_MAX = 256
N_RNN_BLOCKS = 8
RNN_BLOCK = 128
CONV_WIDTH = 4
LRU_C = 8.0
LN_EPS = 1e-5
DEPTH = 1
ALPHA = (2.0 * DEPTH) ** 0.25
NEG_INF = -1e30
MASK_DIST = 1e30

VMEM_LIMIT = 56 * 1024 * 1024


def _resident(shape, index_map):
    return pl.BlockSpec(shape, index_map, pipeline_mode=pl.Buffered(1))


def _layer_norm(y, g, b):
    mu = jnp.mean(y, axis=-1, keepdims=True)
    d = y - mu
    var = jnp.mean(d * d, axis=-1, keepdims=True)
    return d * lax.rsqrt(var + LN_EPS) * g + b


def _dot(a, b):
    return jnp.dot(a, b, preferred_element_type=F32)


def _dot_nt(a, b):
    return lax.dot_general(a, b, (((1,), (1,)), ((), ())), preferred_element_type=F32)


FF_CHUNK = 256


def _ffn_ln_kernel(x_ref, wg_ref, wu_ref, wd_ref, g_ref, b_ref, o_ref, acc_ref):
    x = x_ref[...]
    xb = x.astype(BF16)
    d_ff = wg_ref.shape[1]
    for c in range(d_ff // FF_CHUNK):
        sl = slice(c * FF_CHUNK, (c + 1) * FF_CHUNK)
        g = _dot(xb, wg_ref[:, sl])
        u = _dot(xb, wu_ref[:, sl])
        hid = (g * jax.nn.sigmoid(g) * u).astype(BF16)
        part = _dot(hid, wd_ref[sl, :])
        if c == 0:
            acc_ref[...] = part
        else:
            acc_ref[...] += part
    y = ALPHA * x + 0.5 * acc_ref[...]
    o_ref[...] = _layer_norm(y, g_ref[...], b_ref[...])


def _ffn_ln(x, wg, wu, wd, g, b, tm=512):
    n, d = x.shape
    d_ff = wg.shape[1]
    assert n % tm == 0 and d_ff % FF_CHUNK == 0
    return pl.pallas_call(
        _ffn_ln_kernel,
        grid=(n // tm,),
        in_specs=[
            pl.BlockSpec((tm, d), lambda i: (i, 0)),
            _resident((d, d_ff), lambda i: (0, 0)),
            _resident((d, d_ff), lambda i: (0, 0)),
            _resident((d_ff, d), lambda i: (0, 0)),
            _resident((1, d), lambda i: (0, 0)),
            _resident((1, d), lambda i: (0, 0)),
        ],
        out_specs=pl.BlockSpec((tm, d), lambda i: (i, 0)),
        out_shape=jax.ShapeDtypeStruct((n, d), F32),
        scratch_shapes=[pltpu.VMEM((tm, d), F32)],
        compiler_params=pltpu.CompilerParams(
            dimension_semantics=("arbitrary",), vmem_limit_bytes=VMEM_LIMIT),
        name="ffn_ln",
    )(x, wg, wu, wd, g, b)


def _in_proj_kernel(h_ref, wqT, wk, wvT, wqiT, wki, wwiT, wrest,
                    bqT, bk, bvT, bqiT, bki, bwiT, brest,
                    qT_o, k_o, vT_o, qiT_o, ki_o, wiT_o, rest_o, *, tk):
    hb = h_ref[...].astype(BF16)
    tm = hb.shape[0]
    k_o[...] = (_dot(hb, wk[...]) + bk[...]).astype(BF16)
    ki_o[...] = (_dot(hb, wki[...]) + bki[...]).astype(BF16)
    rest_o[...] = _dot(hb, wrest[...]) + brest[...]
    qT_o[...] = ((_dot_nt(wqT[...], hb) + bqT[...]) * (HEAD_DIM ** -0.5 * LOG2E)).astype(BF16)
    qiT_o[...] = (_dot_nt(wqiT[...], hb) + bqiT[...]).astype(BF16)
    wiT_o[...] = (_dot_nt(wwiT[...], hb) + bwiT[...]) * (N_IDX_HEADS ** -0.5 * IDX_DIM ** -0.5)
    vT = (_dot_nt(wvT[...], hb) + bvT[...]).astype(BF16)
    for c in range(tm // tk):
        vT_o[c] = vT[:, c * tk:(c + 1) * tk]


def _in_proj(h, ws, bs, batch, seq, tk, tm=256):
    n, d = h.shape
    nt = seq // tm
    d_attn = N_HEADS * HEAD_DIM
    d_qi = N_IDX_HEADS * IDX_DIM
    d_rest = ws[6].shape[1]
    row = lambda b, i: (b * nt + i, 0)
    col = lambda b, i: (b, 0, i)
    const = lambda b, i: (0, 0)
    in_specs = [pl.BlockSpec((tm, d), row)]
    in_specs += [_resident(w.shape, const) for w in ws]
    in_specs += [_resident(b_.shape, const) for b_ in bs]
    out_shape = [
        jax.ShapeDtypeStruct((batch, d_attn, seq), BF16),
        jax.ShapeDtypeStruct((n, d_attn), BF16),
        jax.ShapeDtypeStruct((batch, seq // tk, d_attn, tk), BF16),
        jax.ShapeDtypeStruct((batch, d_qi, seq), BF16),
        jax.ShapeDtypeStruct((n, IDX_DIM), BF16),
        jax.ShapeDtypeStruct((batch, N_IDX_HEADS, seq), F32),
        jax.ShapeDtypeStruct((n, d_rest), F32),
    ]
    out_specs = [
        pl.BlockSpec((None, d_attn, tm), col),
        pl.BlockSpec((tm, d_attn), row),
        pl.BlockSpec((None, tm // tk, d_attn, tk), lambda b, i: (b, i, 0, 0)),
        pl.BlockSpec((None, d_qi, tm), col),
        pl.BlockSpec((tm, IDX_DIM), row),
        pl.BlockSpec((None, N_IDX_HEADS, tm), col),
        pl.BlockSpec((tm, d_rest), row),
    ]
    return pl.pallas_call(
        functools.partial(_in_proj_kernel, tk=tk),
        grid=(batch, nt),
        in_specs=in_specs,
        out_specs=out_specs,
        out_shape=out_shape,
        compiler_params=pltpu.CompilerParams(
            dimension_semantics=("arbitrary", "arbitrary"), vmem_limit_bytes=VMEM_LIMIT),
        name="in_proj",
    )(h, *ws, *bs)


COUNT_ROWS = 64
MAX_SEARCH_ITERS = 400
LOG2E = 1.4426950408889634
POS_RADIX = 64
N_SLOPE_PIECES = 3


def _bf16_pieces(c):
    out = []
    for _ in range(N_SLOPE_PIECES):
        piece = float(np.asarray(c, np.float32).astype(BF16).astype(np.float32))
        out.append(piece)
        c = c - piece
    return out


def _dsa_kernel(qT_ref, qiT_ref, wiT_ref, ki_ref, k_ref, vT_ref, o_ref,
                sc_ref, gmax_ref, m_ref, l_ref, acc_ref, cut_ref, qaug_ref, s_ref, *, tq, tk, topk):
    i = pl.program_id(1)
    nkb = ((i + 1) * tq + tk - 1) // tk
    q_pos = i * tq + lax.broadcasted_iota(I32, (tk, tq), 1)
    key_row = lax.broadcasted_iota(I32, (tk, tq), 0)
    q_row = i * tq + lax.broadcasted_iota(I32, (1, tq), 1)

    def score_block(j, carry):
        k0 = pl.multiple_of(j * tk, tk)
        ki = ki_ref[pl.ds(k0, tk), :]
        acc = jnp.zeros((tk, tq), F32)
        for h in range(N_IDX_HEADS):
            r = _dot(ki, qiT_ref[h * IDX_DIM:(h + 1) * IDX_DIM, :])
            acc = acc + wiT_ref[h:h + 1, :] * jnp.maximum(r, 0.0)
        acc = jnp.where(j * tk + key_row <= q_pos, acc, NEG_INF)
        sc_ref[j] = acc

        @pl.when(j == 0)
        def _():
            gmax_ref[...] = acc

        @pl.when(j > 0)
        def _():
            gmax_ref[...] = jnp.maximum(gmax_ref[...], acc)

        return carry

    lax.fori_loop(0, nkb, score_block, 0)

    def count_where(pred):
        def body(j, acc):
            m = jnp.where(pred(sc_ref[j], j), 1, 0).astype(I32)
            for r in range(tk // COUNT_ROWS):
                acc = acc + m[r * COUNT_ROWS:(r + 1) * COUNT_ROWS, :]
            return acc
        acc = lax.fori_loop(0, nkb, body, jnp.zeros((COUNT_ROWS, tq), I32))
        return jnp.sum(acc, axis=0, keepdims=True)

    def count_ge(v):
        return count_where(lambda x, j: x >= v)

    gmax = gmax_ref[...]
    lo = jnp.min(gmax, axis=0, keepdims=True)
    ub = jnp.max(gmax, axis=0, keepdims=True)
    hi = ub + jnp.maximum(jnp.abs(ub) * 2.0 ** -20, 1e-30)
    all_rows = q_row + 1 <= topk
    lo = jnp.where(all_rows, NEG_INF, lo)
    c_lo = jnp.where(all_rows, topk, count_ge(lo))
    c_hi = jnp.zeros((1, tq), I32)
    done = (c_lo == topk).astype(I32)

    def search_cond(st):
        it, lo, hi, c_lo, c_hi, done = st
        return jnp.logical_and(it < MAX_SEARCH_ITERS, jnp.min(done) == 0)

    def search_step(st):
        it, lo, hi, c_lo, c_hi, done = st
        frac = (c_lo - topk).astype(F32) + 0.5
        frac = frac / jnp.maximum(c_lo - c_hi, 1).astype(F32)
        v_int = lo + frac * (hi - lo)
        v_bis = 0.5 * lo + 0.5 * hi
        use_int = jnp.logical_and(it % 2 == 0, jnp.logical_and(v_int > lo, v_int < hi))
        v = jnp.where(use_int, v_int, v_bis)
        adjacent = jnp.logical_or(v_bis <= lo, v_bis >= hi)
        c = count_ge(v)
        live = jnp.logical_and(done == 0, jnp.logical_not(adjacent))
        up = jnp.logical_and(live, c >= topk)
        dn = jnp.logical_and(live, c < topk)
        lo = jnp.where(up, v, lo)
        c_lo = jnp.where(up, c, c_lo)
        hi = jnp.where(dn, v, hi)
        c_hi = jnp.where(dn, c, c_hi)
        done = jnp.where(jnp.logical_or(adjacent, c_lo == topk), 1, done)
        return it + 1, lo, hi, c_lo, c_hi, done

    _, thr, _, n_ge, _, _ = lax.while_loop(
        search_cond, search_step, (jnp.int32(0), lo, hi, c_lo, c_hi, done))

    cut_ref[...] = jnp.full((8, tq), 2 ** 30, I32)

    @pl.when(jnp.max(n_ge) > topk)
    def _():
        need = topk - count_where(lambda x, j: x > thr)

        idx_bits = (sc_ref.shape[0] * tk - 1).bit_length()

        def idx_step(t, c):
            cand = c | jnp.left_shift(jnp.int32(1), idx_bits - 1 - t)
            below = count_where(lambda x, j: jnp.logical_and(x == thr, j * tk + key_row < cand))
            return jnp.where(below < need, cand, c)

        cut = lax.fori_loop(0, idx_bits, idx_step, jnp.zeros((1, tq), I32))
        cut_ref[...] = jnp.broadcast_to(cut, (8, tq))

    cut = cut_ref[0:1, :]

    def mask_block(j, carry):
        x = sc_ref[j]
        k_pos = j * tk + key_row
        sel = jnp.logical_or(x > thr, jnp.logical_and(x == thr, k_pos <= cut))
        sel = jnp.logical_and(sel, k_pos <= q_pos)
        sc_ref[j] = jnp.where(sel, 0.0, NEG_INF)
        return carry

    lax.fori_loop(0, nkb, mask_block, 0)

    row_a = lax.broadcasted_iota(I32, (HEAD_DIM, tq), 0)
    for h in range(N_HEADS):
        hs = slice(h * HEAD_DIM, (h + 1) * HEAD_DIM)
        aug = jnp.zeros((HEAD_DIM, tq), F32)
        for r, c in enumerate(_bf16_pieces(2.0 ** (-8.0 * (h + 1) / N_HEADS) * LOG2E)):
            aug = jnp.where(row_a == 2 * r, c * POS_RADIX, jnp.where(row_a == 2 * r + 1, c, aug))
        qaug_ref[h, 0:HEAD_DIM, :] = qT_ref[hs, :]
        qaug_ref[h, HEAD_DIM:, :] = aug.astype(BF16)

    m_ref[...] = jnp.full(m_ref.shape, -jnp.inf, F32)
    l_ref[...] = jnp.zeros(l_ref.shape, F32)
    acc_ref[...] = jnp.zeros(acc_ref.shape, F32)
    lane_p = lax.broadcasted_iota(I32, (tk, HEAD_DIM), 1)
    row_p = lax.broadcasted_iota(I32, (tk, HEAD_DIM), 0)

    def attn_block(j, carry):
        k0 = pl.multiple_of(j * tk, tk)
        bias = sc_ref[j]
        kp = k0 + row_p
        pos = jnp.where(lane_p % 2 == 0, kp // POS_RADIX, kp % POS_RADIX)
        pos = jnp.where(lane_p < 2 * N_SLOPE_PIECES, pos, 0).astype(F32).astype(BF16)
        m_blk = []
        for h in range(N_HEADS):
            hs = slice(h * HEAD_DIM, (h + 1) * HEAD_DIM)
            lhs = jnp.concatenate([k_ref[pl.ds(k0, tk), hs], pos], axis=1)
            s = _dot(lhs, qaug_ref[h]) + bias
            s_ref[h] = s
            m_blk.append(jnp.max(s, axis=0, keepdims=True))
        m_old = m_ref[...]
        m_new = jnp.maximum(m_old, jnp.concatenate(m_blk, axis=0))
        alpha = jnp.exp2(m_old - m_new)
        m_ref[...] = m_new
        sums = []
        for h in range(N_HEADS):
            hs = slice(h * HEAD_DIM, (h + 1) * HEAD_DIM)
            p = jnp.exp2(s_ref[h] - m_new[h:h + 1, :])
            sums.append(jnp.sum(p, axis=0, keepdims=True))
            pv = _dot(vT_ref[j, hs, :], p.astype(BF16))
            acc_ref[hs, :] = alpha[h:h + 1, :] * acc_ref[hs, :] + pv
        l_ref[...] = alpha * l_ref[...] + jnp.concatenate(sums, axis=0)
        return carry

    lax.fori_loop(0, nkb, attn_block, 0)

    inv_l = 1.0 / l_ref[...]
    for h in range(N_HEADS):
        hs = slice(h * HEAD_DIM, (h + 1) * HEAD_DIM)
        o_ref[:, hs] = (acc_ref[hs, :] * inv_l[h:h + 1, :]).T.astype(BF16)


def _dsa(qT, qiT, wiT, ki, k, vT, batch, seq, tq, tk):
    d_attn = qT.shape[1]
    nq = seq // tq
    topk = min(TOPK_MAX, seq // 4)
    assert tk >= topk and tk % COUNT_ROWS == 0 and seq % tk == 0 and seq % tq == 0
    col = lambda b, i: (b, 0, i)
    kernel = functools.partial(_dsa_kernel, tq=tq, tk=tk, topk=topk)
    return pl.pallas_call(
        kernel,
        grid=(batch, nq),
        in_specs=[
            pl.BlockSpec((None, d_attn, tq), col),
            pl.BlockSpec((None, qiT.shape[1], tq), col),
            pl.BlockSpec((None, N_IDX_HEADS, tq), col),
            _resident((seq, IDX_DIM), lambda b, i: (b, 0)),
            _resident((seq, d_attn), lambda b, i: (b, 0)),
            _resident((None, seq // tk, d_attn, tk), lambda b, i: (b, 0, 0, 0)),
        ],
        out_specs=pl.BlockSpec((tq, d_attn), lambda b, i: (b * nq + i, 0)),
        out_shape=jax.ShapeDtypeStruct((batch * seq, d_attn), BF16),
        scratch_shapes=[
            pltpu.VMEM((seq // tk, tk, tq), F32),
            pltpu.VMEM((tk, tq), F32),
            pltpu.VMEM((N_HEADS, tq), F32),
            pltpu.VMEM((N_HEADS, tq), F32),
            pltpu.VMEM((d_attn, tq), F32),
            pltpu.VMEM((8, tq), I32),
            pltpu.VMEM((N_HEADS, 2 * HEAD_DIM, tq), BF16),
            pltpu.VMEM((N_HEADS, tk, tq), F32),
        ],
        compiler_params=pltpu.CompilerParams(
            dimension_semantics=("arbitrary", "arbitrary"), vmem_limit_bytes=VMEM_LIMIT),
        name="dsa",
    )(qT, qiT, wiT, ki, k, vT)


CONV_HALO = 8


def _rglru_kernel(xr_ref, gr_ref, cw_ref, cb_ref, wg_ref, ba_ref, bx_ref, lam_ref, o_ref,
                  xpad_ref, a_ref, b_ref, h_ref, *, tr):
    i = pl.program_id(1)

    @pl.when(i == 0)
    def _():
        xpad_ref[0:CONV_HALO, :] = jnp.zeros((CONV_HALO, xpad_ref.shape[1]), F32)
        h_ref[...] = jnp.zeros_like(h_ref)

    x = xr_ref[...]
    xpad_ref[CONV_HALO:, :] = x
    xc = cb_ref[...] + cw_ref[CONV_WIDTH - 1:CONV_WIDTH, :] * x
    for j in range(CONV_WIDTH - 1):
        back = CONV_WIDTH - 1 - j
        xc = xc + cw_ref[j:j + 1, :] * xpad_ref[CONV_HALO - back:CONV_HALO - back + tr, :]
    xpad_ref[0:CONV_HALO, :] = x[tr - CONV_HALO:, :]

    xcb = xc.astype(BF16)
    sp = jax.nn.softplus(-lam_ref[...])
    for n in range(N_RNN_BLOCKS):
        cs = slice(n * RNN_BLOCK, (n + 1) * RNN_BLOCK)
        g2 = _dot(xcb[:, cs], wg_ref[n])
        r = jax.nn.sigmoid(g2[:, :RNN_BLOCK] + ba_ref[:, cs])
        ig = jax.nn.sigmoid(g2[:, RNN_BLOCK:] + bx_ref[:, cs])
        log_a = -LRU_C * r * sp[:, cs]
        a = jnp.exp(log_a)
        one_minus_a2 = jnp.tanh(-log_a) * (a * a + 1.0)
        a_ref[:, cs] = a
        b_ref[:, cs] = jnp.sqrt(one_minus_a2) * (ig * xc[:, cs])

    def scan_rows(g, h):
        r0 = pl.multiple_of(g * 8, 8)
        a8 = a_ref[pl.ds(r0, 8), :]
        b8 = b_ref[pl.ds(r0, 8), :]
        rows = []
        for r in range(8):
            h = a8[r:r + 1, :] * h + b8[r:r + 1, :]
            rows.append(h)
        a_ref[pl.ds(r0, 8), :] = jnp.concatenate(rows, axis=0)
        return h

    h_ref[...] = lax.fori_loop(0, tr // 8, scan_rows, h_ref[...])
    o_ref[...] = (a_ref[...] * jax.nn.gelu(gr_ref[...], approximate=True)).astype(BF16)


def _rglru(rest, cw, cb, wg, ba, bx, lam, batch, seq, tr=256):
    n = rest.shape[0]
    d = cw.shape[1]
    nt = seq // tr
    const = lambda b, i: (0, 0)
    return pl.pallas_call(
        functools.partial(_rglru_kernel, tr=tr),
        grid=(batch, nt),
        in_specs=[
            pl.BlockSpec((tr, d), lambda b, i: (b * nt + i, 0)),
            pl.BlockSpec((tr, d), lambda b, i: (b * nt + i, 1)),
            _resident(cw.shape, const),
            _resident(cb.shape, const),
            _resident(wg.shape, lambda b, i: (0, 0, 0)),
            _resident(ba.shape, const),
            _resident(bx.shape, const),
            _resident(lam.shape, const),
        ],
        out_specs=pl.BlockSpec((tr, d), lambda b, i: (b * nt + i, 0)),
        out_shape=jax.ShapeDtypeStruct((n, d), BF16),
        scratch_shapes=[pltpu.VMEM((tr + CONV_HALO, d), F32), pltpu.VMEM((tr, d), F32),
                        pltpu.VMEM((tr, d), F32), pltpu.VMEM((1, d), F32)],
        compiler_params=pltpu.CompilerParams(
            dimension_semantics=("arbitrary", "arbitrary"), vmem_limit_bytes=VMEM_LIMIT),
        name="rglru",
    )(rest, rest, cw, cb, wg, ba, bx, lam)


def _merge_ln_kernel(h_ref, oa_ref, or_ref, ga_ref, gb_ref, wpa, wpr, wout, g_ref, b_ref, o_ref):
    ya = _dot(oa_ref[...], wpa[...])
    yr = _dot(or_ref[...], wpr[...])
    y = jax.nn.sigmoid(ga_ref[...]) * ya + jax.nn.sigmoid(gb_ref[...]) * yr
    mix = _dot(y.astype(BF16), wout[...])
    o_ref[...] = _layer_norm(ALPHA * h_ref[...] + mix, g_ref[...], b_ref[...])


def _merge_ln(h, o_attn, o_rnn, rest, wpa, wpr, wout, g, b, tm=512):
    n, d = h.shape
    const = lambda i: (0, 0)
    return pl.pallas_call(
        _merge_ln_kernel,
        grid=(n // tm,),
        in_specs=[
            pl.BlockSpec((tm, d), lambda i: (i, 0)),
            pl.BlockSpec((tm, o_attn.shape[1]), lambda i: (i, 0)),
            pl.BlockSpec((tm, o_rnn.shape[1]), lambda i: (i, 0)),
            pl.BlockSpec((tm, d), lambda i: (i, 2)),
            pl.BlockSpec((tm, d), lambda i: (i, 3)),
            _resident(wpa.shape, const),
            _resident(wpr.shape, const),
            _resident(wout.shape, const),
            _resident((1, d), const),
            _resident((1, d), const),
        ],
        out_specs=pl.BlockSpec((tm, d), lambda i: (i, 0)),
        out_shape=jax.ShapeDtypeStruct((n, d), F32),
        compiler_params=pltpu.CompilerParams(
            dimension_semantics=("arbitrary",), vmem_limit_bytes=VMEM_LIMIT),
        name="merge_ln",
    )(h, o_attn, o_rnn, rest, rest, wpa, wpr, wout, g, b)


def _layer(x2, batch, seq, p, tq=256, tk=256):
    d_attn = N_HEADS * HEAD_DIM
    row = lambda a: a.reshape(1, -1).astype(F32)
    colv = lambda a: a.reshape(-1, 1).astype(F32)

    h1 = _ffn_ln(x2, p['ffn1_w_gate'].astype(BF16), p['ffn1_w_up'].astype(BF16),
                 p['ffn1_w_down'].astype(BF16), row(p['ln1_g']), row(p['ln1_b']))

    w_in, b_in = p['w_in'], p['b_in']
    o = 0
    wq, bq = w_in[:, o:o + d_attn], b_in[o:o + d_attn]; o += d_attn
    wk, bk = w_in[:, o:o + d_attn], b_in[o:o + d_attn]; o += d_attn
    wv, bv = w_in[:, o:o + d_attn], b_in[o:o + d_attn]; o += d_attn
    n_qi = N_IDX_HEADS * IDX_DIM
    wqi, bqi = w_in[:, o:o + n_qi], b_in[o:o + n_qi]; o += n_qi
    wki, bki = w_in[:, o:o + IDX_DIM], b_in[o:o + IDX_DIM]; o += IDX_DIM
    wwi, bwi = w_in[:, o:o + N_IDX_HEADS], b_in[o:o + N_IDX_HEADS]; o += N_IDX_HEADS
    wrest, brest = w_in[:, o:], b_in[o:]
    ws = [w.astype(BF16) for w in (wq.T, wk, wv.T, wqi.T, wki, wwi.T, wrest)]
    bs = [colv(bq), row(bk), colv(bv), colv(bqi), row(bki), colv(bwi), row(brest)]
    qT, k, vT, qiT, ki, wiT, rest = _in_proj(h1, ws, bs, batch, seq, tk)

    o_attn = _dsa(qT, qiT, wiT, ki, k, vT, batch, seq, tq, tk)

    w_gates = jnp.concatenate([p['lru_w_a'], p['lru_w_x']], axis=-1).astype(BF16)
    o_rnn = _rglru(rest, p['conv_w'].astype(F32), row(p['conv_b']), w_gates, row(p['lru_b_a']),
                   row(p['lru_b_x']), row(p['lru_lambda']), batch, seq)

    h2 = _merge_ln(h1, o_attn, o_rnn, rest, p['w_proj_attn'].astype(BF16),
                   p['w_proj_rnn'].astype(BF16), p['w_out'].astype(BF16),
                   row(p['ln2_g']), row(p['ln2_b']))

    return _ffn_ln(h2, p['ffn2_w_gate'].astype(BF16), p['ffn2_w_up'].astype(BF16),
                   p['ffn2_w_down'].astype(BF16), row(p['ln3_g']), row(p['ln3_b']))


_PARAM_NAMES = ('ln1_g', 'ln1_b', 'ffn1_w_gate', 'ffn1_w_up', 'ffn1_w_down', 'w_in', 'b_in', 'conv_w',
                'conv_b', 'lru_w_a', 'lru_b_a', 'lru_w_x', 'lru_b_x', 'lru_lambda', 'w_proj_attn',
                'w_proj_rnn', 'w_out', 'ln2_g', 'ln2_b', 'ffn2_w_gate', 'ffn2_w_up', 'ffn2_w_down',
                'ln3_g', 'ln3_b')


@jax.jit
def _forward(x, *params):
    batch, seq, d = x.shape
    h = x.reshape(batch * seq, d)
    for l in range(DEPTH):
        p = {name: a[l] for name, a in zip(_PARAM_NAMES, params)}
        h = _layer(h, batch, seq, p)
    return h.reshape(batch, seq, d)


def kernel(x, ln1_g, ln1_b, ffn1_w_gate, ffn1_w_up, ffn1_w_down, w_in, b_in, conv_w, conv_b, lru_w_a, lru_b_a, lru_w_x, lru_b_x, lru_lambda, w_proj_attn, w_proj_rnn, w_out, ln2_g, ln2_b, ffn2_w_gate, ffn2_w_up, ffn2_w_down, ln3_g, ln3_b):
    return _forward(x, ln1_g, ln1_b, ffn1_w_gate, ffn1_w_up, ffn1_w_down, w_in, b_in, conv_w, conv_b,
                    lru_w_a, lru_b_a, lru_w_x, lru_b_x, lru_lambda, w_proj_attn, w_proj_rnn, w_out,
                    ln2_g, ln2_b, ffn2_w_gate, ffn2_w_up, ffn2_w_down, ln3_g, ln3_b)
```

```python
import functools

import jax
import jax.numpy as jnp
import numpy as np
from jax import lax
from jax.experimental import pallas as pl
from jax.experimental.pallas import tpu as pltpu

F32 = jnp.float32
BF16 = jnp.bfloat16
I32 = jnp.int32

N_HEADS = 8
HEAD_DIM = 128
N_IDX_HEADS = 16
IDX_DIM = 64
TOPK_MAX = 256
N_RNN_BLOCKS = 8
RNN_BLOCK = 128
CONV_WIDTH = 4
LRU_C = 8.0
LN_EPS = 1e-5
DEPTH = 1
ALPHA = (2.0 * DEPTH) ** 0.25
NEG_INF = -1e30

VMEM_LIMIT = 56 * 1024 * 1024


def _resident(shape, index_map):
    return pl.BlockSpec(shape, index_map, pipeline_mode=pl.Buffered(1))


def _layer_norm(y, g, b):
    mu = jnp.mean(y, axis=-1, keepdims=True)
    d = y - mu
    var = jnp.mean(d * d, axis=-1, keepdims=True)
    return d * lax.rsqrt(var + LN_EPS) * g + b


def _dot(a, b):
    return jnp.dot(a, b, preferred_element_type=F32)


def _dot_nt(a, b):
    return lax.dot_general(a, b, (((1,), (1,)), ((), ())), preferred_element_type=F32)


FF_CHUNK = 256


def _ffn_ln_kernel(x_ref, wg_ref, wu_ref, wd_ref, g_ref, b_ref, o_ref, acc_ref):
    x = x_ref[...]
    xb = x.astype(BF16)
    d_ff = wg_ref.shape[1]
    for c in range(d_ff // FF_CHUNK):
        sl = slice(c * FF_CHUNK, (c + 1) * FF_CHUNK)
        g = _dot(xb, wg_ref[:, sl])
        u = _dot(xb, wu_ref[:, sl])
        hid = (g * jax.nn.sigmoid(g) * u).astype(BF16)
        part = _dot(hid, wd_ref[sl, :])
        if c == 0:
            acc_ref[...] = part
        else:
            acc_ref[...] += part
    y = ALPHA * x + 0.5 * acc_ref[...]
    o_ref[...] = _layer_norm(y, g_ref[...], b_ref[...])


def _ffn_ln(x, wg, wu, wd, g, b, tm=512):
    n, d = x.shape
    d_ff = wg.shape[1]
    assert n % tm == 0 and d_ff % FF_CHUNK == 0
    return pl.pallas_call(
        _ffn_ln_kernel,
        grid=(n // tm,),
        in_specs=[
            pl.BlockSpec((tm, d), lambda i: (i, 0)),
            _resident((d, d_ff), lambda i: (0, 0)),
            _resident((d, d_ff), lambda i: (0, 0)),
            _resident((d_ff, d), lambda i: (0, 0)),
            _resident((1, d), lambda i: (0, 0)),
            _resident((1, d), lambda i: (0, 0)),
        ],
        out_specs=pl.BlockSpec((tm, d), lambda i: (i, 0)),
        out_shape=jax.ShapeDtypeStruct((n, d), F32),
        scratch_shapes=[pltpu.VMEM((tm, d), F32)],
        compiler_params=pltpu.CompilerParams(
            dimension_semantics=("arbitrary",), vmem_limit_bytes=VMEM_LIMIT),
        name="ffn_ln",
    )(x, wg, wu, wd, g, b)


def _in_proj_kernel(h_ref, wqT, wk, wvT, wqiT, wki, wwiT, wrest,
                    bqT, bk, bvT, bqiT, bki, bwiT, brest,
                    qT_o, k_o, vT_o, qiT_o, ki_o, wiT_o, rest_o, *, tk):
    hb = h_ref[...].astype(BF16)
    tm = hb.shape[0]
    k_o[...] = (_dot(hb, wk[...]) + bk[...]).astype(BF16)
    ki_o[...] = (_dot(hb, wki[...]) + bki[...]).astype(BF16)
    rest_o[...] = _dot(hb, wrest[...]) + brest[...]
    qT_o[...] = ((_dot_nt(wqT[...], hb) + bqT[...]) * (HEAD_DIM ** -0.5 * LOG2E)).astype(BF16)
    qiT_o[...] = (_dot_nt(wqiT[...], hb) + bqiT[...]).astype(BF16)
    wiT_o[...] = (_dot_nt(wwiT[...], hb) + bwiT[...]) * (N_IDX_HEADS ** -0.5 * IDX_DIM ** -0.5)
    vT = (_dot_nt(wvT[...], hb) + bvT[...]).astype(BF16)
    for c in range(tm // tk):
        vT_o[c] = vT[:, c * tk:(c + 1) * tk]


def _in_proj(h, ws, bs, batch, seq, tk, tm=256):
    n, d = h.shape
    nt = seq // tm
    d_attn = N_HEADS * HEAD_DIM
    d_qi = N_IDX_HEADS * IDX_DIM
    d_rest = ws[6].shape[1]
    row = lambda b, i: (b * nt + i, 0)
    col = lambda b, i: (b, 0, i)
    const = lambda b, i: (0, 0)
    in_specs = [pl.BlockSpec((tm, d), row)]
    in_specs += [_resident(w.shape, const) for w in ws]
    in_specs += [_resident(b_.shape, const) for b_ in bs]
    out_shape = [
        jax.ShapeDtypeStruct((batch, d_attn, seq), BF16),
        jax.ShapeDtypeStruct((n, d_attn), BF16),
        jax.ShapeDtypeStruct((batch, seq // tk, d_attn, tk), BF16),
        jax.ShapeDtypeStruct((batch, d_qi, seq), BF16),
        jax.ShapeDtypeStruct((n, IDX_DIM), BF16),
        jax.ShapeDtypeStruct((batch, N_IDX_HEADS, seq), F32),
        jax.ShapeDtypeStruct((n, d_rest), F32),
    ]
    out_specs = [
        pl.BlockSpec((None, d_attn, tm), col),
        pl.BlockSpec((tm, d_attn), row),
        pl.BlockSpec((None, tm // tk, d_attn, tk), lambda b, i: (b, i, 0, 0)),
        pl.BlockSpec((None, d_qi, tm), col),
        pl.BlockSpec((tm, IDX_DIM), row),
        pl.BlockSpec((None, N_IDX_HEADS, tm), col),
        pl.BlockSpec((tm, d_rest), row),
    ]
    return pl.pallas_call(
        functools.partial(_in_proj_kernel, tk=tk),
        grid=(batch, nt),
        in_specs=in_specs,
        out_specs=out_specs,
        out_shape=out_shape,
        compiler_params=pltpu.CompilerParams(
            dimension_semantics=("arbitrary", "arbitrary"), vmem_limit_bytes=VMEM_LIMIT),
        name="in_proj",
    )(h, *ws, *bs)


COUNT_ROWS = 64
MIN_SEARCH_ITERS = 8
MAX_SEARCH_ITERS = 400
LOG2E = 1.4426950408889634
POS_RADIX = 64
N_SLOPE_PIECES = 3


def _bf16_pieces(c):
    out = []
    for _ in range(N_SLOPE_PIECES):
        piece = float(np.asarray(c, np.float32).astype(BF16).astype(np.float32))
        out.append(piece)
        c = c - piece
    return out


def _dsa_kernel(qT_ref, qiT_ref, wiT_ref, ki_ref, k_ref, vT_ref, o_ref,
                sc_ref, gmax_ref, l_ref, acc_ref, cut_ref, qaug_ref, s_ref, *, tq, tk, topk):
    i = pl.program_id(1)
    nkb = ((i + 1) * tq + tk - 1) // tk
    q_pos = i * tq + lax.broadcasted_iota(I32, (tk, tq), 1)
    key_row = lax.broadcasted_iota(I32, (tk, tq), 0)
    q_row = i * tq + lax.broadcasted_iota(I32, (1, tq), 1)

    def score_block(j, carry):
        k0 = pl.multiple_of(j * tk, tk)
        ki = ki_ref[pl.ds(k0, tk), :]
        acc = jnp.zeros((tk, tq), F32)
        for h in range(N_IDX_HEADS):
            r = _dot(ki, qiT_ref[h * IDX_DIM:(h + 1) * IDX_DIM, :])
            acc = acc + wiT_ref[h:h + 1, :] * jnp.maximum(r, 0.0)
        acc = jnp.where(j * tk + key_row <= q_pos, acc, NEG_INF)
        sc_ref[j] = acc

        @pl.when(j == 0)
        def _():
            gmax_ref[...] = acc

        @pl.when(j > 0)
        def _():
            gmax_ref[...] = jnp.maximum(gmax_ref[...], acc)

        return carry

    lax.fori_loop(0, nkb, score_block, 0)

    def count_where(pred):
        def body(j, acc):
            m = jnp.where(pred(sc_ref[j], j), 1, 0).astype(I32)
            for r in range(tk // COUNT_ROWS):
                acc = acc + m[r * COUNT_ROWS:(r + 1) * COUNT_ROWS, :]
            return acc
        acc = lax.fori_loop(0, nkb, body, jnp.zeros((COUNT_ROWS, tq), I32))
        return jnp.sum(acc, axis=0, keepdims=True)

    def count_ge(v):
        return count_where(lambda x, j: x >= v)

    gmax = gmax_ref[...]
    lo = jnp.min(gmax, axis=0, keepdims=True)
    ub = jnp.max(gmax, axis=0, keepdims=True)
    hi = ub + jnp.maximum(jnp.abs(ub) * 2.0 ** -20, 1e-30)
    all_rows = q_row + 1 <= topk
    lo = jnp.where(all_rows, NEG_INF, lo)
    c_lo = jnp.where(all_rows, topk, count_ge(lo))
    done = (c_lo == topk).astype(I32)

    def search_cond(st):
        it, lo, hi, c_lo, done = st
        return jnp.logical_and(it < MAX_SEARCH_ITERS, jnp.min(done) == 0)

    def search_step(st):
        it, lo, hi, c_lo, done = st
        v = 0.5 * lo + 0.5 * hi
        adjacent = jnp.logical_or(v <= lo, v >= hi)
        c = count_ge(v)
        live = jnp.logical_and(done == 0, jnp.logical_not(adjacent))
        up = jnp.logical_and(live, c >= topk)
        dn = jnp.logical_and(live, c < topk)
        lo = jnp.where(up, v, lo)
        c_lo = jnp.where(up, c, c_lo)
        hi = jnp.where(dn, v, hi)
        done = jnp.where(jnp.logical_or(adjacent, c_lo == topk), 1, done)
        return it + 1, lo, hi, c_lo, done

    st = (jnp.int32(0), lo, hi, c_lo, done)
    st = lax.fori_loop(0, MIN_SEARCH_ITERS, lambda t, st: search_step(st), st)
    _, thr, _, n_ge, _ = lax.while_loop(search_cond, search_step, st)

    cut_ref[...] = jnp.full((8, tq), 2 ** 30, I32)

    @pl.when(jnp.max(n_ge) > topk)
    def _():
        need = topk - count_where(lambda x, j: x > thr)

        idx_bits = (sc_ref.shape[0] * tk - 1).bit_length()

        def idx_step(t, c):
            cand = c | jnp.left_shift(jnp.int32(1), idx_bits - 1 - t)
            below = count_where(lambda x, j: jnp.logical_and(x == thr, j * tk + key_row < cand))
            return jnp.where(below < need, cand, c)

        cut = lax.fori_loop(0, idx_bits, idx_step, jnp.zeros((1, tq), I32))
        cut_ref[...] = jnp.broadcast_to(cut, (8, tq))

    cut = cut_ref[0:1, :]

    def mask_block(j, carry):
        x = sc_ref[j]
        k_pos = j * tk + key_row
        sel = jnp.logical_or(x > thr, jnp.logical_and(x == thr, k_pos <= cut))
        sel = jnp.logical_and(sel, k_pos <= q_pos)
        sc_ref[j] = jnp.where(sel, 0.0, NEG_INF)
        return carry

    lax.fori_loop(0, nkb, mask_block, 0)

    row_a = lax.broadcasted_iota(I32, (HEAD_DIM, tq), 0)
    for h in range(N_HEADS):
        hs = slice(h * HEAD_DIM, (h + 1) * HEAD_DIM)
        aug = jnp.zeros((HEAD_DIM, tq), F32)
        for r, c in enumerate(_bf16_pieces(2.0 ** (-8.0 * (h + 1) / N_HEADS) * LOG2E)):
            aug = jnp.where(row_a == 2 * r, c * POS_RADIX, jnp.where(row_a == 2 * r + 1, c, aug))
        qaug_ref[h, 0:HEAD_DIM, :] = qT_ref[hs, :]
        qaug_ref[h, HEAD_DIM:, :] = aug.astype(BF16)

    l_ref[...] = jnp.zeros(l_ref.shape, F32)
    acc_ref[...] = jnp.zeros(acc_ref.shape, F32)
    lane_p = lax.broadcasted_iota(I32, (tk, HEAD_DIM), 1)
    row_p = lax.broadcasted_iota(I32, (tk, HEAD_DIM), 0)

    def logits_block(j, m_run, slot):
        k0 = pl.multiple_of(j * tk, tk)
        bias = sc_ref[j]
        kp = k0 + row_p
        pos = jnp.where(lane_p % 2 == 0, kp // POS_RADIX, kp % POS_RADIX)
        pos = jnp.where(lane_p < 2 * N_SLOPE_PIECES, pos, 0).astype(F32).astype(BF16)
        m_blk = []
        for h in range(N_HEADS):
            hs = slice(h * HEAD_DIM, (h + 1) * HEAD_DIM)
            lhs = jnp.concatenate([k_ref[pl.ds(k0, tk), hs], pos], axis=1)
            s = _dot(lhs, qaug_ref[h]) + bias
            s_ref[slot, h] = s
            m_blk.append(jnp.max(s, axis=0, keepdims=True))
        m_new = jnp.maximum(m_run, jnp.concatenate(m_blk, axis=0))
        return m_new, jnp.exp2(m_run - m_new)

    def values_block(j, m_j, alpha, slot):
        sums = []
        for h in range(N_HEADS):
            hs = slice(h * HEAD_DIM, (h + 1) * HEAD_DIM)
            p = jnp.exp2(s_ref[slot, h] - m_j[h:h + 1, :])
            sums.append(jnp.sum(p, axis=0, keepdims=True))
            pv = _dot(vT_ref[j, hs, :], p.astype(BF16))
            acc_ref[hs, :] = alpha[h:h + 1, :] * acc_ref[hs, :] + pv
        l_ref[...] = alpha * l_ref[...] + jnp.concatenate(sums, axis=0)

    def attn_step(j, carry, slot):
        m_j, alpha_j = carry
        nxt = logits_block(j + 1, m_j, 1 - slot)
        values_block(j, m_j, alpha_j, slot)
        return nxt

    def attn_pair(i, carry):
        return attn_step(2 * i + 1, attn_step(2 * i, carry, 0), 1)

    first = logits_block(0, jnp.full((N_HEADS, tq), -jnp.inf, F32), 0)
    n_pairs = (nkb - 1) // 2
    m_c, alpha_c = lax.fori_loop(0, n_pairs, attn_pair, first)
    j_rest = 2 * n_pairs

    @pl.when(j_rest == nkb - 1)
    def _():
        values_block(j_rest, m_c, alpha_c, 0)

    @pl.when(j_rest < nkb - 1)
    def _():
        m_l, alpha_l = attn_step(j_rest, (m_c, alpha_c), 0)
        values_block(j_rest + 1, m_l, alpha_l, 1)

    inv_l = 1.0 / l_ref[...]
    for h in range(N_HEADS):
        hs = slice(h * HEAD_DIM, (h + 1) * HEAD_DIM)
        o_ref[:, hs] = (acc_ref[hs, :] * inv_l[h:h + 1, :]).T.astype(BF16)


def _dsa(qT, qiT, wiT, ki, k, vT, batch, seq, tq, tk):
    d_attn = qT.shape[1]
    nq = seq // tq
    topk = min(TOPK_MAX, seq // 4)
    assert tk >= topk and tk % COUNT_ROWS == 0 and seq % tk == 0 and seq % tq == 0
    col = lambda b, i: (b, 0, i)
    kernel = functools.partial(_dsa_kernel, tq=tq, tk=tk, topk=topk)
    return pl.pallas_call(
        kernel,
        grid=(batch, nq),
        in_specs=[
            pl.BlockSpec((None, d_attn, tq), col),
            pl.BlockSpec((None, qiT.shape[1], tq), col),
            pl.BlockSpec((None, N_IDX_HEADS, tq), col),
            _resident((seq, IDX_DIM), lambda b, i: (b, 0)),
            _resident((seq, d_attn), lambda b, i: (b, 0)),
            _resident((None, seq // tk, d_attn, tk), lambda b, i: (b, 0, 0, 0)),
        ],
        out_specs=pl.BlockSpec((tq, d_attn), lambda b, i: (b * nq + i, 0)),
        out_shape=jax.ShapeDtypeStruct((batch * seq, d_attn), BF16),
        scratch_shapes=[
            pltpu.VMEM((seq // tk, tk, tq), F32),
            pltpu.VMEM((tk, tq), F32),
            pltpu.VMEM((N_HEADS, tq), F32),
            pltpu.VMEM((d_attn, tq), F32),
            pltpu.VMEM((8, tq), I32),
            pltpu.VMEM((N_HEADS, 2 * HEAD_DIM, tq), BF16),
            pltpu.VMEM((2, N_HEADS, tk, tq), F32),
        ],
        compiler_params=pltpu.CompilerParams(
            dimension_semantics=("arbitrary", "arbitrary"), vmem_limit_bytes=VMEM_LIMIT),
        name="dsa",
    )(qT, qiT, wiT, ki, k, vT)


CONV_HALO = 8


def _rglru_kernel(xr_ref, gr_ref, cw_ref, cb_ref, wg_ref, ba_ref, bx_ref, lam_ref, o_ref,
                  xpad_ref, a_ref, b_ref, h_ref, *, tr):
    i = pl.program_id(1)

    @pl.when(i == 0)
    def _():
        xpad_ref[0:CONV_HALO, :] = jnp.zeros((CONV_HALO, xpad_ref.shape[1]), F32)
        h_ref[...] = jnp.zeros_like(h_ref)

    x = xr_ref[...]
    xpad_ref[CONV_HALO:, :] = x
    xc = cb_ref[...] + cw_ref[CONV_WIDTH - 1:CONV_WIDTH, :] * x
    for j in range(CONV_WIDTH - 1):
        back = CONV_WIDTH - 1 - j
        xc = xc + cw_ref[j:j + 1, :] * xpad_ref[CONV_HALO - back:CONV_HALO - back + tr, :]
    xpad_ref[0:CONV_HALO, :] = x[tr - CONV_HALO:, :]

    xcb = xc.astype(BF16)
    sp = jax.nn.softplus(-lam_ref[...])
    for n in range(N_RNN_BLOCKS):
        cs = slice(n * RNN_BLOCK, (n + 1) * RNN_BLOCK)
        g2 = _dot(xcb[:, cs], wg_ref[n])
        r = jax.nn.sigmoid(g2[:, :RNN_BLOCK] + ba_ref[:, cs])
        ig = jax.nn.sigmoid(g2[:, RNN_BLOCK:] + bx_ref[:, cs])
        log_a = -LRU_C * r * sp[:, cs]
        a = jnp.exp(log_a)
        one_minus_a2 = jnp.tanh(-log_a) * (a * a + 1.0)
        a_ref[:, cs] = a
        b_ref[:, cs] = jnp.sqrt(one_minus_a2) * (ig * xc[:, cs])

    def scan_rows(g, h):
        r0 = pl.multiple_of(g * 8, 8)
        a8 = a_ref[pl.ds(r0, 8), :]
        b8 = b_ref[pl.ds(r0, 8), :]
        rows = []
        for r in range(8):
            h = a8[r:r + 1, :] * h + b8[r:r + 1, :]
            rows.append(h)
        a_ref[pl.ds(r0, 8), :] = jnp.concatenate(rows, axis=0)
        return h

    h_ref[...] = lax.fori_loop(0, tr // 8, scan_rows, h_ref[...])
    o_ref[...] = (a_ref[...] * jax.nn.gelu(gr_ref[...], approximate=True)).astype(BF16)


def _rglru(rest, cw, cb, wg, ba, bx, lam, batch, seq, tr=256):
    n = rest.shape[0]
    d = cw.shape[1]
    nt = seq // tr
    const = lambda b, i: (0, 0)
    return pl.pallas_call(
        functools.partial(_rglru_kernel, tr=tr),
        grid=(batch, nt),
        in_specs=[
            pl.BlockSpec((tr, d), lambda b, i: (b * nt + i, 0)),
            pl.BlockSpec((tr, d), lambda b, i: (b * nt + i, 1)),
            _resident(cw.shape, const),
            _resident(cb.shape, const),
            _resident(wg.shape, lambda b, i: (0, 0, 0)),
            _resident(ba.shape, const),
            _resident(bx.shape, const),
            _resident(lam.shape, const),
        ],
        out_specs=pl.BlockSpec((tr, d), lambda b, i: (b * nt + i, 0)),
        out_shape=jax.ShapeDtypeStruct((n, d), BF16),
        scratch_shapes=[pltpu.VMEM((tr + CONV_HALO, d), F32), pltpu.VMEM((tr, d), F32),
                        pltpu.VMEM((tr, d), F32), pltpu.VMEM((1, d), F32)],
        compiler_params=pltpu.CompilerParams(
            dimension_semantics=("arbitrary", "arbitrary"), vmem_limit_bytes=VMEM_LIMIT),
        name="rglru",
    )(rest, rest, cw, cb, wg, ba, bx, lam)


def _merge_ln_kernel(h_ref, oa_ref, or_ref, ga_ref, gb_ref, wpa, wpr, wout, g_ref, b_ref, o_ref):
    ya = _dot(oa_ref[...], wpa[...])
    yr = _dot(or_ref[...], wpr[...])
    y = jax.nn.sigmoid(ga_ref[...]) * ya + jax.nn.sigmoid(gb_ref[...]) * yr
    mix = _dot(y.astype(BF16), wout[...])
    o_ref[...] = _layer_norm(ALPHA * h_ref[...] + mix, g_ref[...], b_ref[...])


def _merge_ln(h, o_attn, o_rnn, rest, wpa, wpr, wout, g, b, tm=512):
    n, d = h.shape
    const = lambda i: (0, 0)
    return pl.pallas_call(
        _merge_ln_kernel,
        grid=(n // tm,),
        in_specs=[
            pl.BlockSpec((tm, d), lambda i: (i, 0)),
            pl.BlockSpec((tm, o_attn.shape[1]), lambda i: (i, 0)),
            pl.BlockSpec((tm, o_rnn.shape[1]), lambda i: (i, 0)),
            pl.BlockSpec((tm, d), lambda i: (i, 2)),
            pl.BlockSpec((tm, d), lambda i: (i, 3)),
            _resident(wpa.shape, const),
            _resident(wpr.shape, const),
            _resident(wout.shape, const),
            _resident((1, d), const),
            _resident((1, d), const),
        ],
        out_specs=pl.BlockSpec((tm, d), lambda i: (i, 0)),
        out_shape=jax.ShapeDtypeStruct((n, d), F32),
        compiler_params=pltpu.CompilerParams(
            dimension_semantics=("arbitrary",), vmem_limit_bytes=VMEM_LIMIT),
        name="merge_ln",
    )(h, o_attn, o_rnn, rest, rest, wpa, wpr, wout, g, b)


def _layer(x2, batch, seq, p, tq=256, tk=256):
    d_attn = N_HEADS * HEAD_DIM
    row = lambda a: a.reshape(1, -1).astype(F32)
    colv = lambda a: a.reshape(-1, 1).astype(F32)

    h1 = _ffn_ln(x2, p['ffn1_w_gate'].astype(BF16), p['ffn1_w_up'].astype(BF16),
                 p['ffn1_w_down'].astype(BF16), row(p['ln1_g']), row(p['ln1_b']))

    w_in, b_in = p['w_in'], p['b_in']
    o = 0
    wq, bq = w_in[:, o:o + d_attn], b_in[o:o + d_attn]; o += d_attn
    wk, bk = w_in[:, o:o + d_attn], b_in[o:o + d_attn]; o += d_attn
    wv, bv = w_in[:, o:o + d_attn], b_in[o:o + d_attn]; o += d_attn
    n_qi = N_IDX_HEADS * IDX_DIM
    wqi, bqi = w_in[:, o:o + n_qi], b_in[o:o + n_qi]; o += n_qi
    wki, bki = w_in[:, o:o + IDX_DIM], b_in[o:o + IDX_DIM]; o += IDX_DIM
    wwi, bwi = w_in[:, o:o + N_IDX_HEADS], b_in[o:o + N_IDX_HEADS]; o += N_IDX_HEADS
    wrest, brest = w_in[:, o:], b_in[o:]
    ws = [w.astype(BF16) for w in (wq.T, wk, wv.T, wqi.T, wki, wwi.T, wrest)]
    bs = [colv(bq), row(bk), colv(bv), colv(bqi), row(bki), colv(bwi), row(brest)]
    qT, k, vT, qiT, ki, wiT, rest = _in_proj(h1, ws, bs, batch, seq, tk)

    o_attn = _dsa(qT, qiT, wiT, ki, k, vT, batch, seq, tq, tk)

    w_gates = jnp.concatenate([p['lru_w_a'], p['lru_w_x']], axis=-1).astype(BF16)
    o_rnn = _rglru(rest, p['conv_w'].astype(F32), row(p['conv_b']), w_gates, row(p['lru_b_a']),
                   row(p['lru_b_x']), row(p['lru_lambda']), batch, seq)

    h2 = _merge_ln(h1, o_attn, o_rnn, rest, p['w_proj_attn'].astype(BF16),
                   p['w_proj_rnn'].astype(BF16), p['w_out'].astype(BF16),
                   row(p['ln2_g']), row(p['ln2_b']))

    return _ffn_ln(h2, p['ffn2_w_gate'].astype(BF16), p['ffn2_w_up'].astype(BF16),
                   p['ffn2_w_down'].astype(BF16), row(p['ln3_g']), row(p['ln3_b']))


_PARAM_NAMES = ('ln1_g', 'ln1_b', 'ffn1_w_gate', 'ffn1_w_up', 'ffn1_w_down', 'w_in', 'b_in', 'conv_w',
                'conv_b', 'lru_w_a', 'lru_b_a', 'lru_w_x', 'lru_b_x', 'lru_lambda', 'w_proj_attn',
                'w_proj_rnn', 'w_out', 'ln2_g', 'ln2_b', 'ffn2_w_gate', 'ffn2_w_up', 'ffn2_w_down',
                'ln3_g', 'ln3_b')


@jax.jit
def _forward(x, *params):
    batch, seq, d = x.shape
    h = x.reshape(batch * seq, d)
    for l in range(DEPTH):
        p = {name: a[l] for name, a in zip(_PARAM_NAMES, params)}
        h = _layer(h, batch, seq, p)
    return h.reshape(batch, seq, d)


def kernel(x, ln1_g, ln1_b, ffn1_w_gate, ffn1_w_up, ffn1_w_down, w_in, b_in, conv_w, conv_b, lru_w_a, lru_b_a, lru_w_x, lru_b_x, lru_lambda, w_proj_attn, w_proj_rnn, w_out, ln2_g, ln2_b, ffn2_w_gate, ffn2_w_up, ffn2_w_down, ln3_g, ln3_b):
    return _forward(x, ln1_g, ln1_b, ffn1_w_gate, ffn1_w_up, ffn1_w_down, w_in, b_in, conv_w, conv_b,
                    lru_w_a, lru_b_a, lru_w_x, lru_b_x, lru_lambda, w_proj_attn, w_proj_rnn, w_out,
                    ln2_g, ln2_b, ffn2_w_gate, ffn2_w_up, ffn2_w_down, ln3_g, ln3_b)
```

```python
import functools

import jax
import jax.numpy as jnp
import numpy as np
from jax import lax
from jax.experimental import pallas as pl
from jax.experimental.pallas import tpu as pltpu

F32 = jnp.float32
BF16 = jnp.bfloat16
I32 = jnp.int32

N_HEADS = 8
HEAD_DIM = 128
N_IDX_HEADS = 16
IDX_DIM = 64
TOPK_MAX = 256
N_RNN_BLOCKS = 8
RNN_BLOCK = 128
CONV_WIDTH = 4
LRU_C = 8.0
LN_EPS = 1e-5
DEPTH = 1
ALPHA = (2.0 * DEPTH) ** 0.25
NEG_INF = -1e30

VMEM_LIMIT = 56 * 1024 * 1024


def _resident(shape, index_map):
    return pl.BlockSpec(shape, index_map, pipeline_mode=pl.Buffered(1))


def _layer_norm(y, g, b):
    mu = jnp.mean(y, axis=-1, keepdims=True)
    d = y - mu
    var = jnp.mean(d * d, axis=-1, keepdims=True)
    return d * lax.rsqrt(var + LN_EPS) * g + b


def _dot(a, b):
    return jnp.dot(a, b, preferred_element_type=F32)


def _dot_nt(a, b):
    return lax.dot_general(a, b, (((1,), (1,)), ((), ())), preferred_element_type=F32)


FF_CHUNK = 256


def _ffn_ln_kernel(x_ref, wg_ref, wu_ref, wd_ref, g_ref, b_ref, o_ref, acc_ref):
    x = x_ref[...]
    xb = x.astype(BF16)
    d_ff = wg_ref.shape[1]
    for c in range(d_ff // FF_CHUNK):
        sl = slice(c * FF_CHUNK, (c + 1) * FF_CHUNK)
        g = _dot(xb, wg_ref[:, sl])
        u = _dot(xb, wu_ref[:, sl])
        hid = (g * jax.nn.sigmoid(g) * u).astype(BF16)
        part = _dot(hid, wd_ref[sl, :])
        if c == 0:
            acc_ref[...] = part
        else:
            acc_ref[...] += part
    y = ALPHA * x + 0.5 * acc_ref[...]
    o_ref[...] = _layer_norm(y, g_ref[...], b_ref[...])


def _ffn_ln(x, wg, wu, wd, g, b, tm=512):
    n, d = x.shape
    d_ff = wg.shape[1]
    assert n % tm == 0 and d_ff % FF_CHUNK == 0
    return pl.pallas_call(
        _ffn_ln_kernel,
        grid=(n // tm,),
        in_specs=[
            pl.BlockSpec((tm, d), lambda i: (i, 0)),
            _resident((d, d_ff), lambda i: (0, 0)),
            _resident((d, d_ff), lambda i: (0, 0)),
            _resident((d_ff, d), lambda i: (0, 0)),
            _resident((1, d), lambda i: (0, 0)),
            _resident((1, d), lambda i: (0, 0)),
        ],
        out_specs=pl.BlockSpec((tm, d), lambda i: (i, 0)),
        out_shape=jax.ShapeDtypeStruct((n, d), F32),
        scratch_shapes=[pltpu.VMEM((tm, d), F32)],
        compiler_params=pltpu.CompilerParams(
            dimension_semantics=("arbitrary",), vmem_limit_bytes=VMEM_LIMIT),
        name="ffn_ln",
    )(x, wg, wu, wd, g, b)


def _in_proj_kernel(h_ref, wqT, wk, wvT, wqiT, wki, wwiT, wrest,
                    bqT, bk, bvT, bqiT, bki, bwiT, brest,
                    qT_o, k_o, vT_o, qiT_o, ki_o, wiT_o, rest_o, *, tk):
    hb = h_ref[...].astype(BF16)
    tm = hb.shape[0]
    k_o[...] = (_dot(hb, wk[...]) + bk[...]).astype(BF16)
    ki_o[...] = (_dot(hb, wki[...]) + bki[...]).astype(BF16)
    rest_o[...] = _dot(hb, wrest[...]) + brest[...]
    qT_o[...] = ((_dot_nt(wqT[...], hb) + bqT[...]) * (HEAD_DIM ** -0.5 * LOG2E)).astype(BF16)
    qiT_o[...] = (_dot_nt(wqiT[...], hb) + bqiT[...]).astype(BF16)
    wiT_o[...] = (_dot_nt(wwiT[...], hb) + bwiT[...]) * (N_IDX_HEADS ** -0.5 * IDX_DIM ** -0.5)
    vT = (_dot_nt(wvT[...], hb) + bvT[...]).astype(BF16)
    for c in range(tm // tk):
        vT_o[c] = vT[:, c * tk:(c + 1) * tk]


def _in_proj(h, ws, bs, batch, seq, tk, tm=256):
    n, d = h.shape
    nt = seq // tm
    d_attn = N_HEADS * HEAD_DIM
    d_qi = N_IDX_HEADS * IDX_DIM
    d_rest = ws[6].shape[1]
    row = lambda b, i: (b * nt + i, 0)
    col = lambda b, i: (b, 0, i)
    const = lambda b, i: (0, 0)
    in_specs = [pl.BlockSpec((tm, d), row)]
    in_specs += [_resident(w.shape, const) for w in ws]
    in_specs += [_resident(b_.shape, const) for b_ in bs]
    out_shape = [
        jax.ShapeDtypeStruct((batch, d_attn, seq), BF16),
        jax.ShapeDtypeStruct((n, d_attn), BF16),
        jax.ShapeDtypeStruct((batch, seq // tk, d_attn, tk), BF16),
        jax.ShapeDtypeStruct((batch, d_qi, seq), BF16),
        jax.ShapeDtypeStruct((n, IDX_DIM), BF16),
        jax.ShapeDtypeStruct((batch, N_IDX_HEADS, seq), F32),
        jax.ShapeDtypeStruct((n, d_rest), F32),
    ]
    out_specs = [
        pl.BlockSpec((None, d_attn, tm), col),
        pl.BlockSpec((tm, d_attn), row),
        pl.BlockSpec((None, tm // tk, d_attn, tk), lambda b, i: (b, i, 0, 0)),
        pl.BlockSpec((None, d_qi, tm), col),
        pl.BlockSpec((tm, IDX_DIM), row),
        pl.BlockSpec((None, N_IDX_HEADS, tm), col),
        pl.BlockSpec((tm, d_rest), row),
    ]
    return pl.pallas_call(
        functools.partial(_in_proj_kernel, tk=tk),
        grid=(batch, nt),
        in_specs=in_specs,
        out_specs=out_specs,
        out_shape=out_shape,
        compiler_params=pltpu.CompilerParams(
            dimension_semantics=("arbitrary", "arbitrary"), vmem_limit_bytes=VMEM_LIMIT),
        name="in_proj",
    )(h, *ws, *bs)


COUNT_ROWS = 64
MIN_SEARCH_ITERS = 8
MAX_SEARCH_ITERS = 400
LOG2E = 1.4426950408889634
POS_RADIX = 64
N_SLOPE_PIECES = 3


def _bf16_pieces(c):
    out = []
    for _ in range(N_SLOPE_PIECES):
        piece = float(np.asarray(c, np.float32).astype(BF16).astype(np.float32))
        out.append(piece)
        c = c - piece
    return out


def _dsa_kernel(qT_ref, qiT_ref, wiT_ref, qiTn_ref, wiTn_ref, ki_ref, k_ref, vT_ref, o_ref,
                sc_ref, gmax_ref, l_ref, acc_ref, cut_ref, qaug_ref, s_ref, pos_ref, *, tq, tk, topk):
    i = pl.program_id(1)
    nkb = i + 1
    q_pos = i * tq + lax.broadcasted_iota(I32, (tk, tq), 1)
    key_row = lax.broadcasted_iota(I32, (tk, tq), 0)
    q_row = i * tq + lax.broadcasted_iota(I32, (1, tq), 1)

    def score_block(j, qi_ref, wi_ref, diag_first_query):
        k0 = pl.multiple_of(j * tk, tk)
        ki = ki_ref[pl.ds(k0, tk), :]
        acc = jnp.zeros((tk, tq), F32)
        for h in range(N_IDX_HEADS):
            r = _dot(ki, qi_ref[h * IDX_DIM:(h + 1) * IDX_DIM, :])
            acc = acc + wi_ref[h:h + 1, :] * jnp.maximum(r, 0.0)
        if diag_first_query is not None:
            acc = jnp.where(k0 + key_row <= diag_first_query + (q_pos - i * tq), acc, NEG_INF)
        sc_ref[j] = acc
        gmax_ref[...] = jnp.maximum(gmax_ref[...], acc)

    def score_next(j):
        score_block(j, qiTn_ref, wiTn_ref, None)

    @pl.when(i == 0)
    def _():
        gmax_ref[...] = jnp.full(gmax_ref.shape, -jnp.inf, F32)
        score_block(0, qiT_ref, wiT_ref, 0)

    def count_where(pred):
        def body(j, acc):
            m = jnp.where(pred(sc_ref[j], j), 1, 0).astype(I32)
            for r in range(tk // COUNT_ROWS):
                acc = acc + m[r * COUNT_ROWS:(r + 1) * COUNT_ROWS, :]
            return acc
        acc = lax.fori_loop(0, nkb, body, jnp.zeros((COUNT_ROWS, tq), I32))
        return jnp.sum(acc, axis=0, keepdims=True)

    def count_ge(v):
        return count_where(lambda x, j: x >= v)

    gmax = gmax_ref[...]
    lo = jnp.min(gmax, axis=0, keepdims=True)
    ub = jnp.max(gmax, axis=0, keepdims=True)
    hi = ub + jnp.maximum(jnp.abs(ub) * 2.0 ** -20, 1e-30)
    all_rows = q_row + 1 <= topk
    lo = jnp.where(all_rows, NEG_INF, lo)
    c_lo = jnp.where(all_rows, topk, count_ge(lo))
    done = (c_lo == topk).astype(I32)

    def search_cond(st):
        it, lo, hi, c_lo, done = st
        return jnp.logical_and(it < MAX_SEARCH_ITERS, jnp.min(done) == 0)

    def search_step(st):
        it, lo, hi, c_lo, done = st
        v = 0.5 * lo + 0.5 * hi
        adjacent = jnp.logical_or(v <= lo, v >= hi)
        c = count_ge(v)
        live = jnp.logical_and(done == 0, jnp.logical_not(adjacent))
        up = jnp.logical_and(live, c >= topk)
        dn = jnp.logical_and(live, c < topk)
        lo = jnp.where(up, v, lo)
        c_lo = jnp.where(up, c, c_lo)
        hi = jnp.where(dn, v, hi)
        done = jnp.where(jnp.logical_or(adjacent, c_lo == topk), 1, done)
        return it + 1, lo, hi, c_lo, done

    st = (jnp.int32(0), lo, hi, c_lo, done)
    st = lax.fori_loop(0, MIN_SEARCH_ITERS, lambda t, st: search_step(st), st)
    _, thr, _, n_ge, _ = lax.while_loop(search_cond, search_step, st)

    cut_ref[...] = jnp.full((8, tq), 2 ** 30, I32)

    @pl.when(jnp.max(n_ge) > topk)
    def _():
        need = topk - count_where(lambda x, j: x > thr)

        idx_bits = (sc_ref.shape[0] * tk - 1).bit_length()

        def idx_step(t, c):
            cand = c | jnp.left_shift(jnp.int32(1), idx_bits - 1 - t)
            below = count_where(lambda x, j: jnp.logical_and(x == thr, j * tk + key_row < cand))
            return jnp.where(below < need, cand, c)

        cut = lax.fori_loop(0, idx_bits, idx_step, jnp.zeros((1, tq), I32))
        cut_ref[...] = jnp.broadcast_to(cut, (8, tq))

    cut = cut_ref[0:1, :]

    def mask_block(j, carry):
        x = sc_ref[j]
        k_pos = j * tk + key_row
        sel = jnp.logical_or(x > thr, jnp.logical_and(x == thr, k_pos <= cut))
        sel = jnp.logical_and(sel, k_pos <= q_pos)
        sc_ref[j] = jnp.where(sel, 0.0, NEG_INF)
        return carry

    lax.fori_loop(0, nkb, mask_block, 0)

    slope_pieces = [_bf16_pieces(2.0 ** (-8.0 * (h + 1) / N_HEADS) * LOG2E) for h in range(N_HEADS)]
    slope_l2 = [sum(pieces) for pieces in slope_pieces]
    row_a = lax.broadcasted_iota(I32, (HEAD_DIM, tq), 0)
    for h in range(N_HEADS):
        hs = slice(h * HEAD_DIM, (h + 1) * HEAD_DIM)
        aug = jnp.zeros((HEAD_DIM, tq), F32)
        for r, c in enumerate(slope_pieces[h]):
            aug = jnp.where(row_a == 2 * r, c * POS_RADIX, jnp.where(row_a == 2 * r + 1, c, aug))
        qaug_ref[h, 0:HEAD_DIM, :] = qT_ref[hs, :]
        qaug_ref[h, HEAD_DIM:, :] = aug.astype(BF16)
    lane_p = lax.broadcasted_iota(I32, (tk, HEAD_DIM), 1)
    row_p = lax.broadcasted_iota(I32, (tk, HEAD_DIM), 0)
    pos = jnp.where(lane_p % 2 == 0, row_p // POS_RADIX, row_p % POS_RADIX)
    pos_ref[...] = jnp.where(lane_p < 2 * N_SLOPE_PIECES, pos, 0).astype(F32).astype(BF16)

    l_ref[...] = jnp.zeros(l_ref.shape, F32)
    acc_ref[...] = jnp.zeros(acc_ref.shape, F32)

    def logits_block(j, bias, m_run, slot):
        k0 = pl.multiple_of(j * tk, tk)
        k0f = jnp.asarray(j * tk).astype(F32)
        m_blk = []
        for h in range(N_HEADS):
            hs = slice(h * HEAD_DIM, (h + 1) * HEAD_DIM)
            lhs = jnp.concatenate([k_ref[pl.ds(k0, tk), hs], pos_ref[...]], axis=1)
            s = _dot(lhs, qaug_ref[h]) + bias
            s_ref[slot, h] = s
            m_blk.append(jnp.max(s, axis=0, keepdims=True) + k0f * slope_l2[h])
        m_new = jnp.maximum(m_run, jnp.concatenate(m_blk, axis=0))
        return m_new, jnp.exp2(m_run - m_new)

    def values_block(j, m_j, alpha, slot):
        sums = []
        for h in range(N_HEADS):
            hs = slice(h * HEAD_DIM, (h + 1) * HEAD_DIM)
            m_local = m_j[h:h + 1, :] - jnp.asarray(j * tk).astype(F32) * slope_l2[h]
            p = jnp.exp2(s_ref[slot, h] - m_local)
            sums.append(jnp.sum(p, axis=0, keepdims=True))
            pv = _dot(vT_ref[j, hs, :], p.astype(BF16))
            acc_ref[hs, :] = alpha[h:h + 1, :] * acc_ref[hs, :] + pv
        l_ref[...] = alpha * l_ref[...] + jnp.concatenate(sums, axis=0)

    def attn_step(j, carry, slot, bias_next):
        m_j, alpha_j = carry
        nxt = logits_block(j + 1, bias_next, m_j, 1 - slot)
        values_block(j, m_j, alpha_j, slot)
        return nxt

    def attn_pair(p, carry):
        bias_a = sc_ref[2 * p + 1]
        bias_b = sc_ref[2 * p + 2]
        carry = attn_step(2 * p, carry, 0, bias_a)
        score_next(2 * p)
        carry = attn_step(2 * p + 1, carry, 1, bias_b)
        score_next(2 * p + 1)
        return carry

    gmax_ref[...] = jnp.full(gmax_ref.shape, -jnp.inf, F32)
    first = logits_block(0, sc_ref[0], jnp.full((N_HEADS, tq), -jnp.inf, F32), 0)
    n_pairs = (nkb - 1) // 2
    m_c, alpha_c = lax.fori_loop(0, n_pairs, attn_pair, first)
    j_rest = 2 * n_pairs

    @pl.when(j_rest == nkb - 1)
    def _():
        values_block(j_rest, m_c, alpha_c, 0)
        score_next(j_rest)

    @pl.when(j_rest < nkb - 1)
    def _():
        m_l, alpha_l = attn_step(j_rest, (m_c, alpha_c), 0, sc_ref[j_rest + 1])
        values_block(j_rest + 1, m_l, alpha_l, 1)
        score_next(j_rest)
        score_next(j_rest + 1)

    @pl.when(i + 1 < pl.num_programs(1))
    def _():
        score_block(nkb, qiTn_ref, wiTn_ref, (i + 1) * tq)

    inv_l = 1.0 / l_ref[...]
    for h in range(N_HEADS):
        hs = slice(h * HEAD_DIM, (h + 1) * HEAD_DIM)
        o_ref[:, hs] = (acc_ref[hs, :] * inv_l[h:h + 1, :]).T.astype(BF16)


def _dsa(qT, qiT, wiT, ki, k, vT, batch, seq, tq, tk):
    d_attn = qT.shape[1]
    nq = seq // tq
    topk = min(TOPK_MAX, seq // 4)
    assert tk >= topk and tk % COUNT_ROWS == 0 and seq % tk == 0 and tq == tk
    col = lambda b, i: (b, 0, i)
    col_next = lambda b, i: (b, 0, jnp.minimum(i + 1, nq - 1))
    kernel = functools.partial(_dsa_kernel, tq=tq, tk=tk, topk=topk)
    return pl.pallas_call(
        kernel,
        grid=(batch, nq),
        in_specs=[
            pl.BlockSpec((None, d_attn, tq), col),
            pl.BlockSpec((None, qiT.shape[1], tq), col),
            pl.BlockSpec((None, N_IDX_HEADS, tq), col),
            pl.BlockSpec((None, qiT.shape[1], tq), col_next),
            pl.BlockSpec((None, N_IDX_HEADS, tq), col_next),
            _resident((seq, IDX_DIM), lambda b, i: (b, 0)),
            _resident((seq, d_attn), lambda b, i: (b, 0)),
            _resident((None, seq // tk, d_attn, tk), lambda b, i: (b, 0, 0, 0)),
        ],
        out_specs=pl.BlockSpec((tq, d_attn), lambda b, i: (b * nq + i, 0)),
        out_shape=jax.ShapeDtypeStruct((batch * seq, d_attn), BF16),
        scratch_shapes=[
            pltpu.VMEM((seq // tk, tk, tq), F32),
            pltpu.VMEM((tk, tq), F32),
            pltpu.VMEM((N_HEADS, tq), F32),
            pltpu.VMEM((d_attn, tq), F32),
            pltpu.VMEM((8, tq), I32),
            pltpu.VMEM((N_HEADS, 2 * HEAD_DIM, tq), BF16),
            pltpu.VMEM((2, N_HEADS, tk, tq), F32),
            pltpu.VMEM((tk, HEAD_DIM), BF16),
        ],
        compiler_params=pltpu.CompilerParams(
            dimension_semantics=("arbitrary", "arbitrary"), vmem_limit_bytes=VMEM_LIMIT),
        name="dsa",
    )(qT, qiT, wiT, qiT, wiT, ki, k, vT)


CONV_HALO = 8


def _rglru_kernel(xr_ref, gr_ref, cw_ref, cb_ref, wg_ref, ba_ref, bx_ref, lam_ref, o_ref,
                  xpad_ref, a_ref, b_ref, h_ref, *, tr):
    i = pl.program_id(1)

    @pl.when(i == 0)
    def _():
        xpad_ref[0:CONV_HALO, :] = jnp.zeros((CONV_HALO, xpad_ref.shape[1]), F32)
        h_ref[...] = jnp.zeros_like(h_ref)

    x = xr_ref[...]
    xpad_ref[CONV_HALO:, :] = x
    xc = cb_ref[...] + cw_ref[CONV_WIDTH - 1:CONV_WIDTH, :] * x
    for j in range(CONV_WIDTH - 1):
        back = CONV_WIDTH - 1 - j
        xc = xc + cw_ref[j:j + 1, :] * xpad_ref[CONV_HALO - back:CONV_HALO - back + tr, :]
    xpad_ref[0:CONV_HALO, :] = x[tr - CONV_HALO:, :]

    xcb = xc.astype(BF16)
    sp = jax.nn.softplus(-lam_ref[...])
    for n in range(N_RNN_BLOCKS):
        cs = slice(n * RNN_BLOCK, (n + 1) * RNN_BLOCK)
        g2 = _dot(xcb[:, cs], wg_ref[n])
        r = jax.nn.sigmoid(g2[:, :RNN_BLOCK] + ba_ref[:, cs])
        ig = jax.nn.sigmoid(g2[:, RNN_BLOCK:] + bx_ref[:, cs])
        log_a = -LRU_C * r * sp[:, cs]
        a = jnp.exp(log_a)
        one_minus_a2 = jnp.tanh(-log_a) * (a * a + 1.0)
        a_ref[:, cs] = a
        b_ref[:, cs] = jnp.sqrt(one_minus_a2) * (ig * xc[:, cs])

    def scan_rows(g, h):
        r0 = pl.multiple_of(g * 8, 8)
        a8 = a_ref[pl.ds(r0, 8), :]
        b8 = b_ref[pl.ds(r0, 8), :]
        rows = []
        for r in range(8):
            h = a8[r:r + 1, :] * h + b8[r:r + 1, :]
            rows.append(h)
        a_ref[pl.ds(r0, 8), :] = jnp.concatenate(rows, axis=0)
        return h

    h_ref[...] = lax.fori_loop(0, tr // 8, scan_rows, h_ref[...])
    o_ref[...] = (a_ref[...] * jax.nn.gelu(gr_ref[...], approximate=True)).astype(BF16)


def _rglru(rest, cw, cb, wg, ba, bx, lam, batch, seq, tr=256):
    n = rest.shape[0]
    d = cw.shape[1]
    nt = seq // tr
    const = lambda b, i: (0, 0)
    return pl.pallas_call(
        functools.partial(_rglru_kernel, tr=tr),
        grid=(batch, nt),
        in_specs=[
            pl.BlockSpec((tr, d), lambda b, i: (b * nt + i, 0)),
            pl.BlockSpec((tr, d), lambda b, i: (b * nt + i, 1)),
            _resident(cw.shape, const),
            _resident(cb.shape, const),
            _resident(wg.shape, lambda b, i: (0, 0, 0)),
            _resident(ba.shape, const),
            _resident(bx.shape, const),
            _resident(lam.shape, const),
        ],
        out_specs=pl.BlockSpec((tr, d), lambda b, i: (b * nt + i, 0)),
        out_shape=jax.ShapeDtypeStruct((n, d), BF16),
        scratch_shapes=[pltpu.VMEM((tr + CONV_HALO, d), F32), pltpu.VMEM((tr, d), F32),
                        pltpu.VMEM((tr, d), F32), pltpu.VMEM((1, d), F32)],
        compiler_params=pltpu.CompilerParams(
            dimension_semantics=("arbitrary", "arbitrary"), vmem_limit_bytes=VMEM_LIMIT),
        name="rglru",
    )(rest, rest, cw, cb, wg, ba, bx, lam)


def _merge_ln_kernel(h_ref, oa_ref, or_ref, ga_ref, gb_ref, wpa, wpr, wout, g_ref, b_ref, o_ref):
    ya = _dot(oa_ref[...], wpa[...])
    yr = _dot(or_ref[...], wpr[...])
    y = jax.nn.sigmoid(ga_ref[...]) * ya + jax.nn.sigmoid(gb_ref[...]) * yr
    mix = _dot(y.astype(BF16), wout[...])
    o_ref[...] = _layer_norm(ALPHA * h_ref[...] + mix, g_ref[...], b_ref[...])


def _merge_ln(h, o_attn, o_rnn, rest, wpa, wpr, wout, g, b, tm=512):
    n, d = h.shape
    const = lambda i: (0, 0)
    return pl.pallas_call(
        _merge_ln_kernel,
        grid=(n // tm,),
        in_specs=[
            pl.BlockSpec((tm, d), lambda i: (i, 0)),
            pl.BlockSpec((tm, o_attn.shape[1]), lambda i: (i, 0)),
            pl.BlockSpec((tm, o_rnn.shape[1]), lambda i: (i, 0)),
            pl.BlockSpec((tm, d), lambda i: (i, 2)),
            pl.BlockSpec((tm, d), lambda i: (i, 3)),
            _resident(wpa.shape, const),
            _resident(wpr.shape, const),
            _resident(wout.shape, const),
            _resident((1, d), const),
            _resident((1, d), const),
        ],
        out_specs=pl.BlockSpec((tm, d), lambda i: (i, 0)),
        out_shape=jax.ShapeDtypeStruct((n, d), F32),
        compiler_params=pltpu.CompilerParams(
            dimension_semantics=("arbitrary",), vmem_limit_bytes=VMEM_LIMIT),
        name="merge_ln",
    )(h, o_attn, o_rnn, rest, rest, wpa, wpr, wout, g, b)


def _layer(x2, batch, seq, p, tq=256, tk=256):
    d_attn = N_HEADS * HEAD_DIM
    row = lambda a: a.reshape(1, -1).astype(F32)
    colv = lambda a: a.reshape(-1, 1).astype(F32)

    h1 = _ffn_ln(x2, p['ffn1_w_gate'].astype(BF16), p['ffn1_w_up'].astype(BF16),
                 p['ffn1_w_down'].astype(BF16), row(p['ln1_g']), row(p['ln1_b']))

    w_in, b_in = p['w_in'], p['b_in']
    o = 0
    wq, bq = w_in[:, o:o + d_attn], b_in[o:o + d_attn]; o += d_attn
    wk, bk = w_in[:, o:o + d_attn], b_in[o:o + d_attn]; o += d_attn
    wv, bv = w_in[:, o:o + d_attn], b_in[o:o + d_attn]; o += d_attn
    n_qi = N_IDX_HEADS * IDX_DIM
    wqi, bqi = w_in[:, o:o + n_qi], b_in[o:o + n_qi]; o += n_qi
    wki, bki = w_in[:, o:o + IDX_DIM], b_in[o:o + IDX_DIM]; o += IDX_DIM
    wwi, bwi = w_in[:, o:o + N_IDX_HEADS], b_in[o:o + N_IDX_HEADS]; o += N_IDX_HEADS
    wrest, brest = w_in[:, o:], b_in[o:]
    ws = [w.astype(BF16) for w in (wq.T, wk, wv.T, wqi.T, wki, wwi.T, wrest)]
    bs = [colv(bq), row(bk), colv(bv), colv(bqi), row(bki), colv(bwi), row(brest)]
    qT, k, vT, qiT, ki, wiT, rest = _in_proj(h1, ws, bs, batch, seq, tk)

    o_attn = _dsa(qT, qiT, wiT, ki, k, vT, batch, seq, tq, tk)

    w_gates = jnp.concatenate([p['lru_w_a'], p['lru_w_x']], axis=-1).astype(BF16)
    o_rnn = _rglru(rest, p['conv_w'].astype(F32), row(p['conv_b']), w_gates, row(p['lru_b_a']),
                   row(p['lru_b_x']), row(p['lru_lambda']), batch, seq)

    h2 = _merge_ln(h1, o_attn, o_rnn, rest, p['w_proj_attn'].astype(BF16),
                   p['w_proj_rnn'].astype(BF16), p['w_out'].astype(BF16),
                   row(p['ln2_g']), row(p['ln2_b']))

    return _ffn_ln(h2, p['ffn2_w_gate'].astype(BF16), p['ffn2_w_up'].astype(BF16),
                   p['ffn2_w_down'].astype(BF16), row(p['ln3_g']), row(p['ln3_b']))


_PARAM_NAMES = ('ln1_g', 'ln1_b', 'ffn1_w_gate', 'ffn1_w_up', 'ffn1_w_down', 'w_in', 'b_in', 'conv_w',
                'conv_b', 'lru_w_a', 'lru_b_a', 'lru_w_x', 'lru_b_x', 'lru_lambda', 'w_proj_attn',
                'w_proj_rnn', 'w_out', 'ln2_g', 'ln2_b', 'ffn2_w_gate', 'ffn2_w_up', 'ffn2_w_down',
                'ln3_g', 'ln3_b')


@jax.jit
def _forward(x, *params):
    batch, seq, d = x.shape
    h = x.reshape(batch * seq, d)
    for l in range(DEPTH):
        p = {name: a[l] for name, a in zip(_PARAM_NAMES, params)}
        h = _layer(h, batch, seq, p)
    return h.reshape(batch, seq, d)


def kernel(x, ln1_g, ln1_b, ffn1_w_gate, ffn1_w_up, ffn1_w_down, w_in, b_in, conv_w, conv_b, lru_w_a, lru_b_a, lru_w_x, lru_b_x, lru_lambda, w_proj_attn, w_proj_rnn, w_out, ln2_g, ln2_b, ffn2_w_gate, ffn2_w_up, ffn2_w_down, ln3_g, ln3_b):
    return _forward(x, ln1_g, ln1_b, ffn1_w_gate, ffn1_w_up, ffn1_w_down, w_in, b_in, conv_w, conv_b,
                    lru_w_a, lru_b_a, lru_w_x, lru_b_x, lru_lambda, w_proj_attn, w_proj_rnn, w_out,
                    ln2_g, ln2_b, ffn2_w_gate, ffn2_w_up, ffn2_w_down, ln3_g, ln3_b)
```

```python
import functools

import jax
import jax.numpy as jnp
import numpy as np
from jax import lax
from jax.experimental import pallas as pl
from jax.experimental.pallas import tpu as pltpu

F32 = jnp.float32
BF16 = jnp.bfloat16
I32 = jnp.int32

N_HEADS = 8
HEAD_DIM = 128
N_IDX_HEADS = 16
IDX_DIM = 64
TOPK_MAX = 256
N_RNN_BLOCKS = 8
RNN_BLOCK = 128
CONV_WIDTH = 4
LRU_C = 8.0
LN_EPS = 1e-5
DEPTH = 1
ALPHA = (2.0 * DEPTH) ** 0.25
NEG_INF = -1e30

VMEM_LIMIT = 56 * 1024 * 1024


def _resident(shape, index_map):
    return pl.BlockSpec(shape, index_map, pipeline_mode=pl.Buffered(1))


def _layer_norm(y, g, b):
    mu = jnp.mean(y, axis=-1, keepdims=True)
    d = y - mu
    var = jnp.mean(d * d, axis=-1, keepdims=True)
    return d * lax.rsqrt(var + LN_EPS) * g + b


def _dot(a, b):
    return jnp.dot(a, b, preferred_element_type=F32)


def _dot_nt(a, b):
    return lax.dot_general(a, b, (((1,), (1,)), ((), ())), preferred_element_type=F32)


FF_CHUNK = 256


def _ffn_ln_kernel(x_ref, wg_ref, wu_ref, wd_ref, g_ref, b_ref, o_ref, acc_ref):
    x = x_ref[...]
    xb = x.astype(BF16)
    d_ff = wg_ref.shape[1]
    for c in range(d_ff // FF_CHUNK):
        sl = slice(c * FF_CHUNK, (c + 1) * FF_CHUNK)
        g = _dot(xb, wg_ref[:, sl])
        u = _dot(xb, wu_ref[:, sl])
        hid = (g * jax.nn.sigmoid(g) * u).astype(BF16)
        part = _dot(hid, wd_ref[sl, :])
        if c == 0:
            acc_ref[...] = part
        else:
            acc_ref[...] += part
    y = ALPHA * x + 0.5 * acc_ref[...]
    o_ref[...] = _layer_norm(y, g_ref[...], b_ref[...])


def _ffn_ln(x, wg, wu, wd, g, b, tm=512):
    n, d = x.shape
    d_ff = wg.shape[1]
    assert n % tm == 0 and d_ff % FF_CHUNK == 0
    return pl.pallas_call(
        _ffn_ln_kernel,
        grid=(n // tm,),
        in_specs=[
            pl.BlockSpec((tm, d), lambda i: (i, 0)),
            _resident((d, d_ff), lambda i: (0, 0)),
            _resident((d, d_ff), lambda i: (0, 0)),
            _resident((d_ff, d), lambda i: (0, 0)),
            _resident((1, d), lambda i: (0, 0)),
            _resident((1, d), lambda i: (0, 0)),
        ],
        out_specs=pl.BlockSpec((tm, d), lambda i: (i, 0)),
        out_shape=jax.ShapeDtypeStruct((n, d), F32),
        scratch_shapes=[pltpu.VMEM((tm, d), F32)],
        compiler_params=pltpu.CompilerParams(
            dimension_semantics=("arbitrary",), vmem_limit_bytes=VMEM_LIMIT),
        name="ffn_ln",
    )(x, wg, wu, wd, g, b)


def _in_proj_kernel(h_ref, wqT, wk, wvT, wqiT, wki, wwiT, wrest,
                    bqT, bk, bvT, bqiT, bki, bwiT, brest,
                    qT_o, k_o, vT_o, qiT_o, ki_o, wiT_o, rest_o, *, tk):
    hb = h_ref[...].astype(BF16)
    tm = hb.shape[0]
    k_o[...] = (_dot(hb, wk[...]) + bk[...]).astype(BF16)
    ki_o[...] = (_dot(hb, wki[...]) + bki[...]).astype(BF16)
    rest_o[...] = _dot(hb, wrest[...]) + brest[...]
    qT_o[...] = ((_dot_nt(wqT[...], hb) + bqT[...]) * (HEAD_DIM ** -0.5 * LOG2E)).astype(BF16)
    qiT_o[...] = (_dot_nt(wqiT[...], hb) + bqiT[...]).astype(BF16)
    wiT_o[...] = (_dot_nt(wwiT[...], hb) + bwiT[...]) * (N_IDX_HEADS ** -0.5 * IDX_DIM ** -0.5)
    vT = (_dot_nt(wvT[...], hb) + bvT[...]).astype(BF16)
    for c in range(tm // tk):
        vT_o[c] = vT[:, c * tk:(c + 1) * tk]


def _in_proj(h, ws, bs, batch, seq, tk, tm=256):
    n, d = h.shape
    nt = seq // tm
    d_attn = N_HEADS * HEAD_DIM
    d_qi = N_IDX_HEADS * IDX_DIM
    d_rest = ws[6].shape[1]
    row = lambda b, i: (b * nt + i, 0)
    col = lambda b, i: (b, 0, i)
    const = lambda b, i: (0, 0)
    in_specs = [pl.BlockSpec((tm, d), row)]
    in_specs += [_resident(w.shape, const) for w in ws]
    in_specs += [_resident(b_.shape, const) for b_ in bs]
    out_shape = [
        jax.ShapeDtypeStruct((batch, d_attn, seq), BF16),
        jax.ShapeDtypeStruct((n, d_attn), BF16),
        jax.ShapeDtypeStruct((batch, seq // tk, d_attn, tk), BF16),
        jax.ShapeDtypeStruct((batch, d_qi, seq), BF16),
        jax.ShapeDtypeStruct((n, IDX_DIM), BF16),
        jax.ShapeDtypeStruct((batch, N_IDX_HEADS, seq), F32),
        jax.ShapeDtypeStruct((n, d_rest), F32),
    ]
    out_specs = [
        pl.BlockSpec((None, d_attn, tm), col),
        pl.BlockSpec((tm, d_attn), row),
        pl.BlockSpec((None, tm // tk, d_attn, tk), lambda b, i: (b, i, 0, 0)),
        pl.BlockSpec((None, d_qi, tm), col),
        pl.BlockSpec((tm, IDX_DIM), row),
        pl.BlockSpec((None, N_IDX_HEADS, tm), col),
        pl.BlockSpec((tm, d_rest), row),
    ]
    return pl.pallas_call(
        functools.partial(_in_proj_kernel, tk=tk),
        grid=(batch, nt),
        in_specs=in_specs,
        out_specs=out_specs,
        out_shape=out_shape,
        compiler_params=pltpu.CompilerParams(
            dimension_semantics=("arbitrary", "arbitrary"), vmem_limit_bytes=VMEM_LIMIT),
        name="in_proj",
    )(h, *ws, *bs)


COUNT_ROWS = 64
MIN_SEARCH_ITERS = 13
MAX_SEARCH_ITERS = 400
LOG2E = 1.4426950408889634
POS_RADIX = 64
N_SLOPE_PIECES = 3


def _bf16_pieces(c):
    out = []
    for _ in range(N_SLOPE_PIECES):
        piece = float(np.asarray(c, np.float32).astype(BF16).astype(np.float32))
        out.append(piece)
        c = c - piece
    return out


def _dsa_kernel(qT_ref, qiT_ref, wiT_ref, qiTn_ref, wiTn_ref, ki_ref, k_ref, vT_ref, o_ref,
                sc_ref, gmax_ref, l_ref, acc_ref, qaug_ref, s_ref, pos_ref, *, tq, tk, topk):
    i = pl.program_id(1)
    nkb = i + 1
    q_pos = i * tq + lax.broadcasted_iota(I32, (tk, tq), 1)
    key_row = lax.broadcasted_iota(I32, (tk, tq), 0)
    q_row = i * tq + lax.broadcasted_iota(I32, (1, tq), 1)

    def score_block(j, qi_ref, wi_ref, diag_first_query):
        k0 = pl.multiple_of(j * tk, tk)
        ki = ki_ref[pl.ds(k0, tk), :]
        acc = jnp.zeros((tk, tq), F32)
        for h in range(N_IDX_HEADS):
            r = _dot(ki, qi_ref[h * IDX_DIM:(h + 1) * IDX_DIM, :])
            acc = acc + wi_ref[h:h + 1, :] * jnp.maximum(r, 0.0)
        if diag_first_query is not None:
            acc = jnp.where(k0 + key_row <= diag_first_query + (q_pos - i * tq), acc, NEG_INF)
        sc_ref[j] = acc
        gmax_ref[...] = jnp.maximum(gmax_ref[...], acc)

    def score_next(j):
        score_block(j, qiTn_ref, wiTn_ref, None)

    @pl.when(i == 0)
    def _():
        gmax_ref[...] = jnp.full(gmax_ref.shape, -jnp.inf, F32)
        score_block(0, qiT_ref, wiT_ref, 0)

    n_pairs_all = (nkb + 1) // 2

    @pl.when(nkb % 2 == 1)
    def _():
        sc_ref[nkb] = jnp.full((tk, tq), NEG_INF, F32)

    def count_where(pred):
        def body(p, acc):
            for j in (2 * p, 2 * p + 1):
                m = jnp.where(pred(sc_ref[j], j), 1, 0).astype(I32)
                for r in range(tk // COUNT_ROWS):
                    acc = acc + m[r * COUNT_ROWS:(r + 1) * COUNT_ROWS, :]
            return acc
        acc = lax.fori_loop(0, n_pairs_all, body, jnp.zeros((COUNT_ROWS, tq), I32))
        return jnp.sum(acc, axis=0, keepdims=True)

    def count_ge(v):
        return count_where(lambda x, j: x >= v)

    gmax = gmax_ref[...]
    lo = jnp.min(gmax, axis=0, keepdims=True)
    ub = jnp.max(gmax, axis=0, keepdims=True)
    hi = ub + jnp.maximum(jnp.abs(ub) * 2.0 ** -20, 1e-30)
    all_rows = q_row + 1 <= topk
    lo = jnp.where(all_rows, NEG_INF, lo)
    c_lo = jnp.where(all_rows, topk, count_ge(lo))
    done = (c_lo == topk).astype(I32)

    def search_cond(st):
        it, lo, hi, c_lo, done = st
        return jnp.logical_and(it < MAX_SEARCH_ITERS, jnp.min(done) == 0)

    def search_step(st):
        it, lo, hi, c_lo, done = st
        v = 0.5 * lo + 0.5 * hi
        adjacent = jnp.logical_or(v <= lo, v >= hi)
        c = count_ge(v)
        live = jnp.logical_and(done == 0, jnp.logical_not(adjacent))
        up = jnp.logical_and(live, c >= topk)
        dn = jnp.logical_and(live, c < topk)
        lo = jnp.where(up, v, lo)
        c_lo = jnp.where(up, c, c_lo)
        hi = jnp.where(dn, v, hi)
        done = jnp.where(jnp.logical_or(adjacent, c_lo == topk), 1, done)
        return it + 1, lo, hi, c_lo, done

    st = (jnp.int32(0), lo, hi, c_lo, done)
    st = lax.fori_loop(0, MIN_SEARCH_ITERS, lambda t, st: search_step(st), st)
    _, thr, _, n_ge, _ = lax.while_loop(search_cond, search_step, st)

    @pl.when(jnp.max(n_ge) > topk)
    def _():
        need = topk - count_where(lambda x, j: x > thr)

        idx_bits = (sc_ref.shape[0] * tk - 1).bit_length()

        def idx_step(t, c):
            cand = c | jnp.left_shift(jnp.int32(1), idx_bits - 1 - t)
            below = count_where(lambda x, j: jnp.logical_and(x == thr, j * tk + key_row < cand))
            return jnp.where(below < need, cand, c)

        cut = lax.fori_loop(0, idx_bits, idx_step, jnp.zeros((1, tq), I32))

        def demote_block(j, carry):
            x = sc_ref[j]
            drop = jnp.logical_and(x == thr, j * tk + key_row > cut)
            sc_ref[j] = jnp.where(drop, NEG_INF, x)
            return carry

        lax.fori_loop(0, nkb, demote_block, 0)

    slope_pieces = [_bf16_pieces(2.0 ** (-8.0 * (h + 1) / N_HEADS) * LOG2E) for h in range(N_HEADS)]
    slope_l2 = [sum(pieces) for pieces in slope_pieces]
    row_a = lax.broadcasted_iota(I32, (HEAD_DIM, tq), 0)
    for h in range(N_HEADS):
        hs = slice(h * HEAD_DIM, (h + 1) * HEAD_DIM)
        aug = jnp.zeros((HEAD_DIM, tq), F32)
        for r, c in enumerate(slope_pieces[h]):
            aug = jnp.where(row_a == 2 * r, c * POS_RADIX, jnp.where(row_a == 2 * r + 1, c, aug))
        qaug_ref[h, 0:HEAD_DIM, :] = qT_ref[hs, :]
        qaug_ref[h, HEAD_DIM:, :] = aug.astype(BF16)
    lane_p = lax.broadcasted_iota(I32, (tk, HEAD_DIM), 1)
    row_p = lax.broadcasted_iota(I32, (tk, HEAD_DIM), 0)
    pos = jnp.where(lane_p % 2 == 0, row_p // POS_RADIX, row_p % POS_RADIX)
    pos_ref[...] = jnp.where(lane_p < 2 * N_SLOPE_PIECES, pos, 0).astype(F32).astype(BF16)

    l_ref[...] = jnp.zeros(l_ref.shape, F32)
    acc_ref[...] = jnp.zeros(acc_ref.shape, F32)

    key_minus_query = key_row - (q_pos - i * tq)

    def logits_block(j, scores, m_run, slot):
        k0 = pl.multiple_of(j * tk, tk)
        k0f = jnp.asarray(j * tk).astype(F32)
        selected = jnp.logical_and(scores >= thr, key_minus_query <= i * tq - j * tk)
        bias = jnp.where(selected, 0.0, NEG_INF)
        m_blk = []
        for h in range(N_HEADS):
            hs = slice(h * HEAD_DIM, (h + 1) * HEAD_DIM)
            lhs = jnp.concatenate([k_ref[pl.ds(k0, tk), hs], pos_ref[...]], axis=1)
            s = _dot(lhs, qaug_ref[h]) + bias
            s_ref[slot, h] = s
            m_blk.append(jnp.max(s, axis=0, keepdims=True) + k0f * slope_l2[h])
        m_new = jnp.maximum(m_run, jnp.concatenate(m_blk, axis=0))
        return m_new, jnp.exp2(m_run - m_new)

    def values_block(j, m_j, alpha, slot):
        sums = []
        for h in range(N_HEADS):
            hs = slice(h * HEAD_DIM, (h + 1) * HEAD_DIM)
            m_local = m_j[h:h + 1, :] - jnp.asarray(j * tk).astype(F32) * slope_l2[h]
            p = jnp.exp2(s_ref[slot, h] - m_local)
            sums.append(jnp.sum(p, axis=0, keepdims=True))
            pv = _dot(vT_ref[j, hs, :], p.astype(BF16))
            acc_ref[hs, :] = alpha[h:h + 1, :] * acc_ref[hs, :] + pv
        l_ref[...] = alpha * l_ref[...] + jnp.concatenate(sums, axis=0)

    def attn_step(j, carry, slot, scores_next):
        m_j, alpha_j = carry
        nxt = logits_block(j + 1, scores_next, m_j, 1 - slot)
        values_block(j, m_j, alpha_j, slot)
        return nxt

    def attn_pair(p, carry):
        scores_a = sc_ref[2 * p + 1]
        scores_b = sc_ref[2 * p + 2]
        carry = attn_step(2 * p, carry, 0, scores_a)
        score_next(2 * p)
        carry = attn_step(2 * p + 1, carry, 1, scores_b)
        score_next(2 * p + 1)
        return carry

    gmax_ref[...] = jnp.full(gmax_ref.shape, -jnp.inf, F32)
    first = logits_block(0, sc_ref[0], jnp.full((N_HEADS, tq), -jnp.inf, F32), 0)
    n_pairs = (nkb - 1) // 2
    m_c, alpha_c = lax.fori_loop(0, n_pairs, attn_pair, first)
    j_rest = 2 * n_pairs

    @pl.when(j_rest == nkb - 1)
    def _():
        values_block(j_rest, m_c, alpha_c, 0)
        score_next(j_rest)

    @pl.when(j_rest < nkb - 1)
    def _():
        m_l, alpha_l = attn_step(j_rest, (m_c, alpha_c), 0, sc_ref[j_rest + 1])
        values_block(j_rest + 1, m_l, alpha_l, 1)
        score_next(j_rest)
        score_next(j_rest + 1)

    @pl.when(i + 1 < pl.num_programs(1))
    def _():
        score_block(nkb, qiTn_ref, wiTn_ref, (i + 1) * tq)

    inv_l = 1.0 / l_ref[...]
    for h in range(N_HEADS):
        hs = slice(h * HEAD_DIM, (h + 1) * HEAD_DIM)
        o_ref[:, hs] = (acc_ref[hs, :] * inv_l[h:h + 1, :]).T.astype(BF16)


def _dsa(qT, qiT, wiT, ki, k, vT, batch, seq, tq, tk):
    d_attn = qT.shape[1]
    nq = seq // tq
    topk = min(TOPK_MAX, seq // 4)
    assert tk >= topk and tk % COUNT_ROWS == 0 and seq % tk == 0 and tq == tk
    col = lambda b, i: (b, 0, i)
    col_next = lambda b, i: (b, 0, jnp.minimum(i + 1, nq - 1))
    kernel = functools.partial(_dsa_kernel, tq=tq, tk=tk, topk=topk)
    return pl.pallas_call(
        kernel,
        grid=(batch, nq),
        in_specs=[
            pl.BlockSpec((None, d_attn, tq), col),
            pl.BlockSpec((None, qiT.shape[1], tq), col),
            pl.BlockSpec((None, N_IDX_HEADS, tq), col),
            pl.BlockSpec((None, qiT.shape[1], tq), col_next),
            pl.BlockSpec((None, N_IDX_HEADS, tq), col_next),
            _resident((seq, IDX_DIM), lambda b, i: (b, 0)),
            _resident((seq, d_attn), lambda b, i: (b, 0)),
            _resident((None, seq // tk, d_attn, tk), lambda b, i: (b, 0, 0, 0)),
        ],
        out_specs=pl.BlockSpec((tq, d_attn), lambda b, i: (b * nq + i, 0)),
        out_shape=jax.ShapeDtypeStruct((batch * seq, d_attn), BF16),
        scratch_shapes=[
            pltpu.VMEM((seq // tk, tk, tq), F32),
            pltpu.VMEM((tk, tq), F32),
            pltpu.VMEM((N_HEADS, tq), F32),
            pltpu.VMEM((d_attn, tq), F32),
            pltpu.VMEM((N_HEADS, 2 * HEAD_DIM, tq), BF16),
            pltpu.VMEM((2, N_HEADS, tk, tq), F32),
            pltpu.VMEM((tk, HEAD_DIM), BF16),
        ],
        compiler_params=pltpu.CompilerParams(
            dimension_semantics=("arbitrary", "arbitrary"), vmem_limit_bytes=VMEM_LIMIT),
        name="dsa",
    )(qT, qiT, wiT, qiT, wiT, ki, k, vT)


CONV_HALO = 8


def _rglru_kernel(xr_ref, gr_ref, cw_ref, cb_ref, wg_ref, ba_ref, bx_ref, lam_ref, o_ref,
                  xpad_ref, a_ref, b_ref, h_ref, *, tr):
    i = pl.program_id(1)

    @pl.when(i == 0)
    def _():
        xpad_ref[0:CONV_HALO, :] = jnp.zeros((CONV_HALO, xpad_ref.shape[1]), F32)
        h_ref[...] = jnp.zeros_like(h_ref)

    x = xr_ref[...]
    xpad_ref[CONV_HALO:, :] = x
    xc = cb_ref[...] + cw_ref[CONV_WIDTH - 1:CONV_WIDTH, :] * x
    for j in range(CONV_WIDTH - 1):
        back = CONV_WIDTH - 1 - j
        xc = xc + cw_ref[j:j + 1, :] * xpad_ref[CONV_HALO - back:CONV_HALO - back + tr, :]
    xpad_ref[0:CONV_HALO, :] = x[tr - CONV_HALO:, :]

    xcb = xc.astype(BF16)
    sp = jax.nn.softplus(-lam_ref[...])
    for n in range(N_RNN_BLOCKS):
        cs = slice(n * RNN_BLOCK, (n + 1) * RNN_BLOCK)
        g2 = _dot(xcb[:, cs], wg_ref[n])
        r = jax.nn.sigmoid(g2[:, :RNN_BLOCK] + ba_ref[:, cs])
        ig = jax.nn.sigmoid(g2[:, RNN_BLOCK:] + bx_ref[:, cs])
        log_a = -LRU_C * r * sp[:, cs]
        a = jnp.exp(log_a)
        one_minus_a2 = jnp.tanh(-log_a) * (a * a + 1.0)
        a_ref[:, cs] = a
        b_ref[:, cs] = jnp.sqrt(one_minus_a2) * (ig * xc[:, cs])

    def scan_rows(g, h):
        r0 = pl.multiple_of(g * 8, 8)
        a8 = a_ref[pl.ds(r0, 8), :]
        b8 = b_ref[pl.ds(r0, 8), :]
        rows = []
        for r in range(8):
            h = a8[r:r + 1, :] * h + b8[r:r + 1, :]
            rows.append(h)
        a_ref[pl.ds(r0, 8), :] = jnp.concatenate(rows, axis=0)
        return h

    h_ref[...] = lax.fori_loop(0, tr // 8, scan_rows, h_ref[...])
    o_ref[...] = (a_ref[...] * jax.nn.gelu(gr_ref[...], approximate=True)).astype(BF16)


def _rglru(rest, cw, cb, wg, ba, bx, lam, batch, seq, tr=256):
    n = rest.shape[0]
    d = cw.shape[1]
    nt = seq // tr
    const = lambda b, i: (0, 0)
    return pl.pallas_call(
        functools.partial(_rglru_kernel, tr=tr),
        grid=(batch, nt),
        in_specs=[
            pl.BlockSpec((tr, d), lambda b, i: (b * nt + i, 0)),
            pl.BlockSpec((tr, d), lambda b, i: (b * nt + i, 1)),
            _resident(cw.shape, const),
            _resident(cb.shape, const),
            _resident(wg.shape, lambda b, i: (0, 0, 0)),
            _resident(ba.shape, const),
            _resident(bx.shape, const),
            _resident(lam.shape, const),
        ],
        out_specs=pl.BlockSpec((tr, d), lambda b, i: (b * nt + i, 0)),
        out_shape=jax.ShapeDtypeStruct((n, d), BF16),
        scratch_shapes=[pltpu.VMEM((tr + CONV_HALO, d), F32), pltpu.VMEM((tr, d), F32),
                        pltpu.VMEM((tr, d), F32), pltpu.VMEM((1, d), F32)],
        compiler_params=pltpu.CompilerParams(
            dimension_semantics=("arbitrary", "arbitrary"), vmem_limit_bytes=VMEM_LIMIT),
        name="rglru",
    )(rest, rest, cw, cb, wg, ba, bx, lam)


def _merge_ln_kernel(h_ref, oa_ref, or_ref, ga_ref, gb_ref, wpa, wpr, wout, g_ref, b_ref, o_ref):
    ya = _dot(oa_ref[...], wpa[...])
    yr = _dot(or_ref[...], wpr[...])
    y = jax.nn.sigmoid(ga_ref[...]) * ya + jax.nn.sigmoid(gb_ref[...]) * yr
    mix = _dot(y.astype(BF16), wout[...])
    o_ref[...] = _layer_norm(ALPHA * h_ref[...] + mix, g_ref[...], b_ref[...])


def _merge_ln(h, o_attn, o_rnn, rest, wpa, wpr, wout, g, b, tm=512):
    n, d = h.shape
    const = lambda i: (0, 0)
    return pl.pallas_call(
        _merge_ln_kernel,
        grid=(n // tm,),
        in_specs=[
            pl.BlockSpec((tm, d), lambda i: (i, 0)),
            pl.BlockSpec((tm, o_attn.shape[1]), lambda i: (i, 0)),
            pl.BlockSpec((tm, o_rnn.shape[1]), lambda i: (i, 0)),
            pl.BlockSpec((tm, d), lambda i: (i, 2)),
            pl.BlockSpec((tm, d), lambda i: (i, 3)),
            _resident(wpa.shape, const),
            _resident(wpr.shape, const),
            _resident(wout.shape, const),
            _resident((1, d), const),
            _resident((1, d), const),
        ],
        out_specs=pl.BlockSpec((tm, d), lambda i: (i, 0)),
        out_shape=jax.ShapeDtypeStruct((n, d), F32),
        compiler_params=pltpu.CompilerParams(
            dimension_semantics=("arbitrary",), vmem_limit_bytes=VMEM_LIMIT),
        name="merge_ln",
    )(h, o_attn, o_rnn, rest, rest, wpa, wpr, wout, g, b)


def _layer(x2, batch, seq, p, tq=256, tk=256):
    d_attn = N_HEADS * HEAD_DIM
    row = lambda a: a.reshape(1, -1).astype(F32)
    colv = lambda a: a.reshape(-1, 1).astype(F32)

    h1 = _ffn_ln(x2, p['ffn1_w_gate'].astype(BF16), p['ffn1_w_up'].astype(BF16),
                 p['ffn1_w_down'].astype(BF16), row(p['ln1_g']), row(p['ln1_b']))

    w_in, b_in = p['w_in'], p['b_in']
    o = 0
    wq, bq = w_in[:, o:o + d_attn], b_in[o:o + d_attn]; o += d_attn
    wk, bk = w_in[:, o:o + d_attn], b_in[o:o + d_attn]; o += d_attn
    wv, bv = w_in[:, o:o + d_attn], b_in[o:o + d_attn]; o += d_attn
    n_qi = N_IDX_HEADS * IDX_DIM
    wqi, bqi = w_in[:, o:o + n_qi], b_in[o:o + n_qi]; o += n_qi
    wki, bki = w_in[:, o:o + IDX_DIM], b_in[o:o + IDX_DIM]; o += IDX_DIM
    wwi, bwi = w_in[:, o:o + N_IDX_HEADS], b_in[o:o + N_IDX_HEADS]; o += N_IDX_HEADS
    wrest, brest = w_in[:, o:], b_in[o:]
    ws = [w.astype(BF16) for w in (wq.T, wk, wv.T, wqi.T, wki, wwi.T, wrest)]
    bs = [colv(bq), row(bk), colv(bv), colv(bqi), row(bki), colv(bwi), row(brest)]
    qT, k, vT, qiT, ki, wiT, rest = _in_proj(h1, ws, bs, batch, seq, tk)

    o_attn = _dsa(qT, qiT, wiT, ki, k, vT, batch, seq, tq, tk)

    w_gates = jnp.concatenate([p['lru_w_a'], p['lru_w_x']], axis=-1).astype(BF16)
    o_rnn = _rglru(rest, p['conv_w'].astype(F32), row(p['conv_b']), w_gates, row(p['lru_b_a']),
                   row(p['lru_b_x']), row(p['lru_lambda']), batch, seq)

    h2 = _merge_ln(h1, o_attn, o_rnn, rest, p['w_proj_attn'].astype(BF16),
                   p['w_proj_rnn'].astype(BF16), p['w_out'].astype(BF16),
                   row(p['ln2_g']), row(p['ln2_b']))

    return _ffn_ln(h2, p['ffn2_w_gate'].astype(BF16), p['ffn2_w_up'].astype(BF16),
                   p['ffn2_w_down'].astype(BF16), row(p['ln3_g']), row(p['ln3_b']))


_PARAM_NAMES = ('ln1_g', 'ln1_b', 'ffn1_w_gate', 'ffn1_w_up', 'ffn1_w_down', 'w_in', 'b_in', 'conv_w',
                'conv_b', 'lru_w_a', 'lru_b_a', 'lru_w_x', 'lru_b_x', 'lru_lambda', 'w_proj_attn',
                'w_proj_rnn', 'w_out', 'ln2_g', 'ln2_b', 'ffn2_w_gate', 'ffn2_w_up', 'ffn2_w_down',
                'ln3_g', 'ln3_b')


@jax.jit
def _forward(x, *params):
    batch, seq, d = x.shape
    h = x.reshape(batch * seq, d)
    for l in range(DEPTH):
        p = {name: a[l] for name, a in zip(_PARAM_NAMES, params)}
        h = _layer(h, batch, seq, p)
    return h.reshape(batch, seq, d)


def kernel(x, ln1_g, ln1_b, ffn1_w_gate, ffn1_w_up, ffn1_w_down, w_in, b_in, conv_w, conv_b, lru_w_a, lru_b_a, lru_w_x, lru_b_x, lru_lambda, w_proj_attn, w_proj_rnn, w_out, ln2_g, ln2_b, ffn2_w_gate, ffn2_w_up, ffn2_w_down, ln3_g, ln3_b):
    return _forward(x, ln1_g, ln1_b, ffn1_w_gate, ffn1_w_up, ffn1_w_down, w_in, b_in, conv_w, conv_b,
                    lru_w_a, lru_b_a, lru_w_x, lru_b_x, lru_lambda, w_proj_attn, w_proj_rnn, w_out,
                    ln2_g, ln2_b, ffn2_w_gate, ffn2_w_up, ffn2_w_down, ln3_g, ln3_b)
```

```python
import functools

import jax
import jax.numpy as jnp
import numpy as np
from jax import lax
from jax.experimental import pallas as pl
from jax.experimental.pallas import tpu as pltpu

F32 = jnp.float32
BF16 = jnp.bfloat16
I32 = jnp.int32

N_HEADS = 8
HEAD_DIM = 128
N_IDX_HEADS = 16
IDX_DIM = 64
TOPK_MAX = 256
N_RNN_BLOCKS = 8
RNN_BLOCK = 128
CONV_WIDTH = 4
LRU_C = 8.0
LN_EPS = 1e-5
DEPTH = 1
ALPHA = (2.0 * DEPTH) ** 0.25
NEG_INF = -1e30

VMEM_LIMIT = 56 * 1024 * 1024


def _resident(shape, index_map):
    return pl.BlockSpec(shape, index_map, pipeline_mode=pl.Buffered(1))


def _layer_norm(y, g, b):
    mu = jnp.mean(y, axis=-1, keepdims=True)
    d = y - mu
    var = jnp.mean(d * d, axis=-1, keepdims=True)
    return d * lax.rsqrt(var + LN_EPS) * g + b


def _dot(a, b):
    return jnp.dot(a, b, preferred_element_type=F32)


def _dot_nt(a, b):
    return lax.dot_general(a, b, (((1,), (1,)), ((), ())), preferred_element_type=F32)


FF_CHUNK = 256


def _ffn_ln_kernel(x_ref, wg_ref, wu_ref, wd_ref, g_ref, b_ref, o_ref, acc_ref):
    x = x_ref[...]
    xb = x.astype(BF16)
    d_ff = wg_ref.shape[1]
    for c in range(d_ff // FF_CHUNK):
        sl = slice(c * FF_CHUNK, (c + 1) * FF_CHUNK)
        g = _dot(xb, wg_ref[:, sl])
        u = _dot(xb, wu_ref[:, sl])
        hid = (g * jax.nn.sigmoid(g) * u).astype(BF16)
        part = _dot(hid, wd_ref[sl, :])
        if c == 0:
            acc_ref[...] = part
        else:
            acc_ref[...] += part
    y = ALPHA * x + 0.5 * acc_ref[...]
    o_ref[...] = _layer_norm(y, g_ref[...], b_ref[...])


def _ffn_ln(x, wg, wu, wd, g, b, tm=512):
    n, d = x.shape
    d_ff = wg.shape[1]
    assert n % tm == 0 and d_ff % FF_CHUNK == 0
    return pl.pallas_call(
        _ffn_ln_kernel,
        grid=(n // tm,),
        in_specs=[
            pl.BlockSpec((tm, d), lambda i: (i, 0)),
            _resident((d, d_ff), lambda i: (0, 0)),
            _resident((d, d_ff), lambda i: (0, 0)),
            _resident((d_ff, d), lambda i: (0, 0)),
            _resident((1, d), lambda i: (0, 0)),
            _resident((1, d), lambda i: (0, 0)),
        ],
        out_specs=pl.BlockSpec((tm, d), lambda i: (i, 0)),
        out_shape=jax.ShapeDtypeStruct((n, d), F32),
        scratch_shapes=[pltpu.VMEM((tm, d), F32)],
        compiler_params=pltpu.CompilerParams(
            dimension_semantics=("arbitrary",), vmem_limit_bytes=VMEM_LIMIT),
        name="ffn_ln",
    )(x, wg, wu, wd, g, b)


def _in_proj_kernel(h_ref, wqT, wk, wvT, wqiT, wki, wwiT, wrest,
                    bqT, bk, bvT, bqiT, bki, bwiT, brest,
                    qT_o, k_o, vT_o, qiT_o, ki_o, wiT_o, rest_o, *, tk):
    hb = h_ref[...].astype(BF16)
    tm = hb.shape[0]
    k_o[...] = (_dot(hb, wk[...]) + bk[...]).astype(BF16)
    ki_o[...] = (_dot(hb, wki[...]) + bki[...]).astype(BF16)
    rest_o[...] = _dot(hb, wrest[...]) + brest[...]
    qT_o[...] = ((_dot_nt(wqT[...], hb) + bqT[...]) * (HEAD_DIM ** -0.5 * LOG2E)).astype(BF16)
    qiT_o[...] = (_dot_nt(wqiT[...], hb) + bqiT[...]).astype(BF16)
    wiT_o[...] = (_dot_nt(wwiT[...], hb) + bwiT[...]) * (N_IDX_HEADS ** -0.5 * IDX_DIM ** -0.5)
    vT = (_dot_nt(wvT[...], hb) + bvT[...]).astype(BF16)
    for c in range(tm // tk):
        vT_o[c] = vT[:, c * tk:(c + 1) * tk]


def _in_proj(h, ws, bs, batch, seq, tk, tm=256):
    n, d = h.shape
    nt = seq // tm
    d_attn = N_HEADS * HEAD_DIM
    d_qi = N_IDX_HEADS * IDX_DIM
    d_rest = ws[6].shape[1]
    row = lambda b, i: (b * nt + i, 0)
    col = lambda b, i: (b, 0, i)
    const = lambda b, i: (0, 0)
    in_specs = [pl.BlockSpec((tm, d), row)]
    in_specs += [_resident(w.shape, const) for w in ws]
    in_specs += [_resident(b_.shape, const) for b_ in bs]
    out_shape = [
        jax.ShapeDtypeStruct((batch, d_attn, seq), BF16),
        jax.ShapeDtypeStruct((n, d_attn), BF16),
        jax.ShapeDtypeStruct((batch, seq // tk, d_attn, tk), BF16),
        jax.ShapeDtypeStruct((batch, d_qi, seq), BF16),
        jax.ShapeDtypeStruct((n, IDX_DIM), BF16),
        jax.ShapeDtypeStruct((batch, N_IDX_HEADS, seq), F32),
        jax.ShapeDtypeStruct((n, d_rest), F32),
    ]
    out_specs = [
        pl.BlockSpec((None, d_attn, tm), col),
        pl.BlockSpec((tm, d_attn), row),
        pl.BlockSpec((None, tm // tk, d_attn, tk), lambda b, i: (b, i, 0, 0)),
        pl.BlockSpec((None, d_qi, tm), col),
        pl.BlockSpec((tm, IDX_DIM), row),
        pl.BlockSpec((None, N_IDX_HEADS, tm), col),
        pl.BlockSpec((tm, d_rest), row),
    ]
    return pl.pallas_call(
        functools.partial(_in_proj_kernel, tk=tk),
        grid=(batch, nt),
        in_specs=in_specs,
        out_specs=out_specs,
        out_shape=out_shape,
        compiler_params=pltpu.CompilerParams(
            dimension_semantics=("arbitrary", "arbitrary"), vmem_limit_bytes=VMEM_LIMIT),
        name="in_proj",
    )(h, *ws, *bs)


COUNT_ROWS = 64
MAX_EXTRACT = 3
MIN_SEARCH_ITERS = 10
MAX_SEARCH_ITERS = 400
LOG2E = 1.4426950408889634
POS_RADIX = 64
N_SLOPE_PIECES = 3


def _bf16_pieces(c):
    out = []
    for _ in range(N_SLOPE_PIECES):
        piece = float(np.asarray(c, np.float32).astype(BF16).astype(np.float32))
        out.append(piece)
        c = c - piece
    return out


def _dsa_kernel(qT_ref, qiT_ref, wiT_ref, qiTn_ref, wiTn_ref, ki_ref, k_ref, vT_ref, o_ref,
                sc_ref, gmax_ref, l_ref, acc_ref, qaug_ref, s_ref, pos_ref, *, tq, tk, topk):
    i = pl.program_id(1)
    nkb = i + 1
    q_pos = i * tq + lax.broadcasted_iota(I32, (tk, tq), 1)
    key_row = lax.broadcasted_iota(I32, (tk, tq), 0)
    q_row = i * tq + lax.broadcasted_iota(I32, (1, tq), 1)

    def score_block(j, qi_ref, wi_ref, diag_first_query):
        k0 = pl.multiple_of(j * tk, tk)
        ki = ki_ref[pl.ds(k0, tk), :]
        acc = jnp.zeros((tk, tq), F32)
        for h in range(N_IDX_HEADS):
            r = _dot(ki, qi_ref[h * IDX_DIM:(h + 1) * IDX_DIM, :])
            acc = acc + wi_ref[h:h + 1, :] * jnp.maximum(r, 0.0)
        if diag_first_query is not None:
            acc = jnp.where(k0 + key_row <= diag_first_query + (q_pos - i * tq), acc, NEG_INF)
        sc_ref[j] = acc
        gmax_ref[...] = jnp.maximum(gmax_ref[...], acc)

    def score_next(j):
        score_block(j, qiTn_ref, wiTn_ref, None)

    @pl.when(i == 0)
    def _():
        gmax_ref[...] = jnp.full(gmax_ref.shape, -jnp.inf, F32)
        score_block(0, qiT_ref, wiT_ref, 0)

    n_pairs_all = (nkb + 1) // 2

    @pl.when(nkb % 2 == 1)
    def _():
        sc_ref[nkb] = jnp.full((tk, tq), NEG_INF, F32)

    def count_where(pred):
        def body(p, acc):
            for j in (2 * p, 2 * p + 1):
                m = jnp.where(pred(sc_ref[j], j), 1, 0).astype(I32)
                for r in range(tk // COUNT_ROWS):
                    acc = acc + m[r * COUNT_ROWS:(r + 1) * COUNT_ROWS, :]
            return acc
        acc = lax.fori_loop(0, n_pairs_all, body, jnp.zeros((COUNT_ROWS, tq), I32))
        return jnp.sum(acc, axis=0, keepdims=True)

    def count_ge(v):
        return count_where(lambda x, j: x >= v)

    gmax = gmax_ref[...]
    lo = jnp.min(gmax, axis=0, keepdims=True)
    ub = jnp.max(gmax, axis=0, keepdims=True)
    hi = ub + jnp.maximum(jnp.abs(ub) * 2.0 ** -20, 1e-30)
    all_rows = q_row + 1 <= topk
    lo = jnp.where(all_rows, NEG_INF, lo)
    c_lo = jnp.where(all_rows, topk, count_ge(lo))
    c_hi = jnp.zeros((1, tq), I32)
    done = (c_lo == topk).astype(I32)

    def search_step(st):
        it, lo, hi, c_lo, c_hi, done = st
        v = 0.5 * lo + 0.5 * hi
        adjacent = jnp.logical_or(v <= lo, v >= hi)
        c = count_ge(v)
        live = jnp.logical_and(done == 0, jnp.logical_not(adjacent))
        up = jnp.logical_and(live, c >= topk)
        dn = jnp.logical_and(live, c < topk)
        lo = jnp.where(up, v, lo)
        c_lo = jnp.where(up, c, c_lo)
        hi = jnp.where(dn, v, hi)
        c_hi = jnp.where(dn, c, c_hi)
        done = jnp.where(jnp.logical_or(adjacent, c_lo == topk), 1, done)
        return it + 1, lo, hi, c_lo, c_hi, done

    def all_done(st):
        return jnp.min(st[5]) == 1

    def near_top(st):
        it, lo, hi, c_lo, c_hi, done = st
        return jnp.min(jnp.where(jnp.logical_or(done == 1, topk - c_hi <= MAX_EXTRACT), 1, 0)) == 1

    def keep_bisecting(stop):
        return lambda st: jnp.logical_and(st[0] < MAX_SEARCH_ITERS, jnp.logical_not(stop(st)))

    st = (jnp.int32(0), lo, hi, c_lo, c_hi, done)
    st = lax.fori_loop(0, MIN_SEARCH_ITERS, lambda t, st: search_step(st), st)
    st = lax.while_loop(keep_bisecting(near_top), search_step, st)
    it, lo, hi, c_lo, c_hi, done = st

    def step_down(t, sd):
        hi_s, c_s = sd
        def body(p, acc):
            for j in (2 * p, 2 * p + 1):
                x = jnp.where(sc_ref[j] < hi_s, sc_ref[j], -jnp.inf)
                for r in range(tk // COUNT_ROWS):
                    acc = jnp.maximum(acc, x[r * COUNT_ROWS:(r + 1) * COUNT_ROWS, :])
            return acc
        acc = lax.fori_loop(0, n_pairs_all, body, jnp.full((COUNT_ROWS, tq), -jnp.inf, F32))
        below = jnp.max(acc, axis=0, keepdims=True)
        move = jnp.logical_and(done == 0, c_s < topk)
        return jnp.where(move, below, hi_s), jnp.where(move, c_s + 1, c_s)

    hi_s, c_s = lax.fori_loop(0, MAX_EXTRACT, step_down, (hi, c_hi))
    stepped = jnp.logical_and(done == 0, c_s == topk)
    c_chk = count_ge(jnp.where(stepped, hi_s, lo))
    lo = jnp.where(stepped, hi_s, lo)
    c_lo = jnp.where(stepped, c_chk, c_lo)
    done = jnp.where(c_lo == topk, 1, done)
    _, thr, _, n_ge, _, _ = lax.while_loop(keep_bisecting(all_done), search_step,
                                           (it, lo, hi, c_lo, c_hi, done))

    @pl.when(jnp.max(n_ge) > topk)
    def _():
        need = topk - count_where(lambda x, j: x > thr)

        idx_bits = (sc_ref.shape[0] * tk - 1).bit_length()

        def idx_step(t, c):
            cand = c | jnp.left_shift(jnp.int32(1), idx_bits - 1 - t)
            below = count_where(lambda x, j: jnp.logical_and(x == thr, j * tk + key_row < cand))
            return jnp.where(below < need, cand, c)

        cut = lax.fori_loop(0, idx_bits, idx_step, jnp.zeros((1, tq), I32))

        def demote_block(j, carry):
            x = sc_ref[j]
            drop = jnp.logical_and(x == thr, j * tk + key_row > cut)
            sc_ref[j] = jnp.where(drop, NEG_INF, x)
            return carry

        lax.fori_loop(0, nkb, demote_block, 0)

    slope_pieces = [_bf16_pieces(2.0 ** (-8.0 * (h + 1) / N_HEADS) * LOG2E) for h in range(N_HEADS)]
    slope_l2 = [sum(pieces) for pieces in slope_pieces]
    row_a = lax.broadcasted_iota(I32, (HEAD_DIM, tq), 0)
    for h in range(N_HEADS):
        hs = slice(h * HEAD_DIM, (h + 1) * HEAD_DIM)
        aug = jnp.zeros((HEAD_DIM, tq), F32)
        for r, c in enumerate(slope_pieces[h]):
            aug = jnp.where(row_a == 2 * r, c * POS_RADIX, jnp.where(row_a == 2 * r + 1, c, aug))
        qaug_ref[h, 0:HEAD_DIM, :] = qT_ref[hs, :]
        qaug_ref[h, HEAD_DIM:, :] = aug.astype(BF16)
    lane_p = lax.broadcasted_iota(I32, (tk, HEAD_DIM), 1)
    row_p = lax.broadcasted_iota(I32, (tk, HEAD_DIM), 0)
    pos = jnp.where(lane_p % 2 == 0, row_p // POS_RADIX, row_p % POS_RADIX)
    pos_ref[...] = jnp.where(lane_p < 2 * N_SLOPE_PIECES, pos, 0).astype(F32).astype(BF16)

    l_ref[...] = jnp.zeros(l_ref.shape, F32)
    acc_ref[...] = jnp.zeros(acc_ref.shape, F32)

    key_minus_query = key_row - (q_pos - i * tq)

    def logits_block(j, scores, m_run, slot):
        k0 = pl.multiple_of(j * tk, tk)
        k0f = jnp.asarray(j * tk).astype(F32)
        selected = jnp.logical_and(scores >= thr, key_minus_query <= i * tq - j * tk)
        bias = jnp.where(selected, 0.0, NEG_INF)
        m_blk = []
        for h in range(N_HEADS):
            hs = slice(h * HEAD_DIM, (h + 1) * HEAD_DIM)
            lhs = jnp.concatenate([k_ref[pl.ds(k0, tk), hs], pos_ref[...]], axis=1)
            s = _dot(lhs, qaug_ref[h]) + bias
            s_ref[slot, h] = s
            m_blk.append(jnp.max(s, axis=0, keepdims=True) + k0f * slope_l2[h])
        m_new = jnp.maximum(m_run, jnp.concatenate(m_blk, axis=0))
        return m_new, jnp.exp2(m_run - m_new)

    def values_block(j, m_j, alpha, slot):
        sums = []
        for h in range(N_HEADS):
            hs = slice(h * HEAD_DIM, (h + 1) * HEAD_DIM)
            m_local = m_j[h:h + 1, :] - jnp.asarray(j * tk).astype(F32) * slope_l2[h]
            p = jnp.exp2(s_ref[slot, h] - m_local)
            sums.append(jnp.sum(p, axis=0, keepdims=True))
            pv = _dot(vT_ref[j, hs, :], p.astype(BF16))
            acc_ref[hs, :] = alpha[h:h + 1, :] * acc_ref[hs, :] + pv
        l_ref[...] = alpha * l_ref[...] + jnp.concatenate(sums, axis=0)

    def attn_step(j, carry, slot, scores_next):
        m_j, alpha_j = carry
        nxt = logits_block(j + 1, scores_next, m_j, 1 - slot)
        values_block(j, m_j, alpha_j, slot)
        return nxt

    def attn_pair(p, carry):
        scores_a = sc_ref[2 * p + 1]
        scores_b = sc_ref[2 * p + 2]
        carry = attn_step(2 * p, carry, 0, scores_a)
        score_next(2 * p)
        carry = attn_step(2 * p + 1, carry, 1, scores_b)
        score_next(2 * p + 1)
        return carry

    gmax_ref[...] = jnp.full(gmax_ref.shape, -jnp.inf, F32)
    first = logits_block(0, sc_ref[0], jnp.full((N_HEADS, tq), -jnp.inf, F32), 0)
    n_pairs = (nkb - 1) // 2
    m_c, alpha_c = lax.fori_loop(0, n_pairs, attn_pair, first)
    j_rest = 2 * n_pairs

    @pl.when(j_rest == nkb - 1)
    def _():
        values_block(j_rest, m_c, alpha_c, 0)
        score_next(j_rest)

    @pl.when(j_rest < nkb - 1)
    def _():
        m_l, alpha_l = attn_step(j_rest, (m_c, alpha_c), 0, sc_ref[j_rest + 1])
        values_block(j_rest + 1, m_l, alpha_l, 1)
        score_next(j_rest)
        score_next(j_rest + 1)

    @pl.when(i + 1 < pl.num_programs(1))
    def _():
        score_block(nkb, qiTn_ref, wiTn_ref, (i + 1) * tq)

    inv_l = 1.0 / l_ref[...]
    for h in range(N_HEADS):
        hs = slice(h * HEAD_DIM, (h + 1) * HEAD_DIM)
        o_ref[:, hs] = (acc_ref[hs, :] * inv_l[h:h + 1, :]).T.astype(BF16)


def _dsa(qT, qiT, wiT, ki, k, vT, batch, seq, tq, tk):
    d_attn = qT.shape[1]
    nq = seq // tq
    topk = min(TOPK_MAX, seq // 4)
    assert tk >= topk and tk % COUNT_ROWS == 0 and seq % tk == 0 and tq == tk
    col = lambda b, i: (b, 0, i)
    col_next = lambda b, i: (b, 0, jnp.minimum(i + 1, nq - 1))
    kernel = functools.partial(_dsa_kernel, tq=tq, tk=tk, topk=topk)
    return pl.pallas_call(
        kernel,
        grid=(batch, nq),
        in_specs=[
            pl.BlockSpec((None, d_attn, tq), col),
            pl.BlockSpec((None, qiT.shape[1], tq), col),
            pl.BlockSpec((None, N_IDX_HEADS, tq), col),
            pl.BlockSpec((None, qiT.shape[1], tq), col_next),
            pl.BlockSpec((None, N_IDX_HEADS, tq), col_next),
            _resident((seq, IDX_DIM), lambda b, i: (b, 0)),
            _resident((seq, d_attn), lambda b, i: (b, 0)),
            _resident((None, seq // tk, d_attn, tk), lambda b, i: (b, 0, 0, 0)),
        ],
        out_specs=pl.BlockSpec((tq, d_attn), lambda b, i: (b * nq + i, 0)),
        out_shape=jax.ShapeDtypeStruct((batch * seq, d_attn), BF16),
        scratch_shapes=[
            pltpu.VMEM((seq // tk, tk, tq), F32),
            pltpu.VMEM((tk, tq), F32),
            pltpu.VMEM((N_HEADS, tq), F32),
            pltpu.VMEM((d_attn, tq), F32),
            pltpu.VMEM((N_HEADS, 2 * HEAD_DIM, tq), BF16),
            pltpu.VMEM((2, N_HEADS, tk, tq), F32),
            pltpu.VMEM((tk, HEAD_DIM), BF16),
        ],
        compiler_params=pltpu.CompilerParams(
            dimension_semantics=("arbitrary", "arbitrary"), vmem_limit_bytes=VMEM_LIMIT),
        name="dsa",
    )(qT, qiT, wiT, qiT, wiT, ki, k, vT)


CONV_HALO = 8


def _rglru_kernel(xr_ref, gr_ref, cw_ref, cb_ref, wg_ref, ba_ref, bx_ref, lam_ref, o_ref,
                  xpad_ref, a_ref, b_ref, h_ref, *, tr):
    i = pl.program_id(1)

    @pl.when(i == 0)
    def _():
        xpad_ref[0:CONV_HALO, :] = jnp.zeros((CONV_HALO, xpad_ref.shape[1]), F32)
        h_ref[...] = jnp.zeros_like(h_ref)

    x = xr_ref[...]
    xpad_ref[CONV_HALO:, :] = x
    xc = cb_ref[...] + cw_ref[CONV_WIDTH - 1:CONV_WIDTH, :] * x
    for j in range(CONV_WIDTH - 1):
        back = CONV_WIDTH - 1 - j
        xc = xc + cw_ref[j:j + 1, :] * xpad_ref[CONV_HALO - back:CONV_HALO - back + tr, :]
    xpad_ref[0:CONV_HALO, :] = x[tr - CONV_HALO:, :]

    xcb = xc.astype(BF16)
    sp = jax.nn.softplus(-lam_ref[...])
    for n in range(N_RNN_BLOCKS):
        cs = slice(n * RNN_BLOCK, (n + 1) * RNN_BLOCK)
        g2 = _dot(xcb[:, cs], wg_ref[n])
        r = jax.nn.sigmoid(g2[:, :RNN_BLOCK] + ba_ref[:, cs])
        ig = jax.nn.sigmoid(g2[:, RNN_BLOCK:] + bx_ref[:, cs])
        log_a = -LRU_C * r * sp[:, cs]
        a = jnp.exp(log_a)
        one_minus_a2 = jnp.tanh(-log_a) * (a * a + 1.0)
        a_ref[:, cs] = a
        b_ref[:, cs] = jnp.sqrt(one_minus_a2) * (ig * xc[:, cs])

    def scan_rows(g, h):
        r0 = pl.multiple_of(g * 8, 8)
        a8 = a_ref[pl.ds(r0, 8), :]
        b8 = b_ref[pl.ds(r0, 8), :]
        rows = []
        for r in range(8):
            h = a8[r:r + 1, :] * h + b8[r:r + 1, :]
            rows.append(h)
        a_ref[pl.ds(r0, 8), :] = jnp.concatenate(rows, axis=0)
        return h

    h_ref[...] = lax.fori_loop(0, tr // 8, scan_rows, h_ref[...])
    o_ref[...] = (a_ref[...] * jax.nn.gelu(gr_ref[...], approximate=True)).astype(BF16)


def _rglru(rest, cw, cb, wg, ba, bx, lam, batch, seq, tr=256):
    n = rest.shape[0]
    d = cw.shape[1]
    nt = seq // tr
    const = lambda b, i: (0, 0)
    return pl.pallas_call(
        functools.partial(_rglru_kernel, tr=tr),
        grid=(batch, nt),
        in_specs=[
            pl.BlockSpec((tr, d), lambda b, i: (b * nt + i, 0)),
            pl.BlockSpec((tr, d), lambda b, i: (b * nt + i, 1)),
            _resident(cw.shape, const),
            _resident(cb.shape, const),
            _resident(wg.shape, lambda b, i: (0, 0, 0)),
            _resident(ba.shape, const),
            _resident(bx.shape, const),
            _resident(lam.shape, const),
        ],
        out_specs=pl.BlockSpec((tr, d), lambda b, i: (b * nt + i, 0)),
        out_shape=jax.ShapeDtypeStruct((n, d), BF16),
        scratch_shapes=[pltpu.VMEM((tr + CONV_HALO, d), F32), pltpu.VMEM((tr, d), F32),
                        pltpu.VMEM((tr, d), F32), pltpu.VMEM((1, d), F32)],
        compiler_params=pltpu.CompilerParams(
            dimension_semantics=("arbitrary", "arbitrary"), vmem_limit_bytes=VMEM_LIMIT),
        name="rglru",
    )(rest, rest, cw, cb, wg, ba, bx, lam)


def _merge_ln_kernel(h_ref, oa_ref, or_ref, ga_ref, gb_ref, wpa, wpr, wout, g_ref, b_ref, o_ref):
    ya = _dot(oa_ref[...], wpa[...])
    yr = _dot(or_ref[...], wpr[...])
    y = jax.nn.sigmoid(ga_ref[...]) * ya + jax.nn.sigmoid(gb_ref[...]) * yr
    mix = _dot(y.astype(BF16), wout[...])
    o_ref[...] = _layer_norm(ALPHA * h_ref[...] + mix, g_ref[...], b_ref[...])


def _merge_ln(h, o_attn, o_rnn, rest, wpa, wpr, wout, g, b, tm=512):
    n, d = h.shape
    const = lambda i: (0, 0)
    return pl.pallas_call(
        _merge_ln_kernel,
        grid=(n // tm,),
        in_specs=[
            pl.BlockSpec((tm, d), lambda i: (i, 0)),
            pl.BlockSpec((tm, o_attn.shape[1]), lambda i: (i, 0)),
            pl.BlockSpec((tm, o_rnn.shape[1]), lambda i: (i, 0)),
            pl.BlockSpec((tm, d), lambda i: (i, 2)),
            pl.BlockSpec((tm, d), lambda i: (i, 3)),
            _resident(wpa.shape, const),
            _resident(wpr.shape, const),
            _resident(wout.shape, const),
            _resident((1, d), const),
            _resident((1, d), const),
        ],
        out_specs=pl.BlockSpec((tm, d), lambda i: (i, 0)),
        out_shape=jax.ShapeDtypeStruct((n, d), F32),
        compiler_params=pltpu.CompilerParams(
            dimension_semantics=("arbitrary",), vmem_limit_bytes=VMEM_LIMIT),
        name="merge_ln",
    )(h, o_attn, o_rnn, rest, rest, wpa, wpr, wout, g, b)


def _layer(x2, batch, seq, p, tq=256, tk=256):
    d_attn = N_HEADS * HEAD_DIM
    row = lambda a: a.reshape(1, -1).astype(F32)
    colv = lambda a: a.reshape(-1, 1).astype(F32)

    h1 = _ffn_ln(x2, p['ffn1_w_gate'].astype(BF16), p['ffn1_w_up'].astype(BF16),
                 p['ffn1_w_down'].astype(BF16), row(p['ln1_g']), row(p['ln1_b']))

    w_in, b_in = p['w_in'], p['b_in']
    o = 0
    wq, bq = w_in[:, o:o + d_attn], b_in[o:o + d_attn]; o += d_attn
    wk, bk = w_in[:, o:o + d_attn], b_in[o:o + d_attn]; o += d_attn
    wv, bv = w_in[:, o:o + d_attn], b_in[o:o + d_attn]; o += d_attn
    n_qi = N_IDX_HEADS * IDX_DIM
    wqi, bqi = w_in[:, o:o + n_qi], b_in[o:o + n_qi]; o += n_qi
    wki, bki = w_in[:, o:o + IDX_DIM], b_in[o:o + IDX_DIM]; o += IDX_DIM
    wwi, bwi = w_in[:, o:o + N_IDX_HEADS], b_in[o:o + N_IDX_HEADS]; o += N_IDX_HEADS
    wrest, brest = w_in[:, o:], b_in[o:]
    ws = [w.astype(BF16) for w in (wq.T, wk, wv.T, wqi.T, wki, wwi.T, wrest)]
    bs = [colv(bq), row(bk), colv(bv), colv(bqi), row(bki), colv(bwi), row(brest)]
    qT, k, vT, qiT, ki, wiT, rest = _in_proj(h1, ws, bs, batch, seq, tk)

    o_attn = _dsa(qT, qiT, wiT, ki, k, vT, batch, seq, tq, tk)

    w_gates = jnp.concatenate([p['lru_w_a'], p['lru_w_x']], axis=-1).astype(BF16)
    o_rnn = _rglru(rest, p['conv_w'].astype(F32), row(p['conv_b']), w_gates, row(p['lru_b_a']),
                   row(p['lru_b_x']), row(p['lru_lambda']), batch, seq)

    h2 = _merge_ln(h1, o_attn, o_rnn, rest, p['w_proj_attn'].astype(BF16),
                   p['w_proj_rnn'].astype(BF16), p['w_out'].astype(BF16),
                   row(p['ln2_g']), row(p['ln2_b']))

    return _ffn_ln(h2, p['ffn2_w_gate'].astype(BF16), p['ffn2_w_up'].astype(BF16),
                   p['ffn2_w_down'].astype(BF16), row(p['ln3_g']), row(p['ln3_b']))


_PARAM_NAMES = ('ln1_g', 'ln1_b', 'ffn1_w_gate', 'ffn1_w_up', 'ffn1_w_down', 'w_in', 'b_in', 'conv_w',
                'conv_b', 'lru_w_a', 'lru_b_a', 'lru_w_x', 'lru_b_x', 'lru_lambda', 'w_proj_attn',
                'w_proj_rnn', 'w_out', 'ln2_g', 'ln2_b', 'ffn2_w_gate', 'ffn2_w_up', 'ffn2_w_down',
                'ln3_g', 'ln3_b')


@jax.jit
def _forward(x, *params):
    batch, seq, d = x.shape
    h = x.reshape(batch * seq, d)
    for l in range(DEPTH):
        p = {name: a[l] for name, a in zip(_PARAM_NAMES, params)}
        h = _layer(h, batch, seq, p)
    return h.reshape(batch, seq, d)


def kernel(x, ln1_g, ln1_b, ffn1_w_gate, ffn1_w_up, ffn1_w_down, w_in, b_in, conv_w, conv_b, lru_w_a, lru_b_a, lru_w_x, lru_b_x, lru_lambda, w_proj_attn, w_proj_rnn, w_out, ln2_g, ln2_b, ffn2_w_gate, ffn2_w_up, ffn2_w_down, ln3_g, ln3_b):
    return _forward(x, ln1_g, ln1_b, ffn1_w_gate, ffn1_w_up, ffn1_w_down, w_in, b_in, conv_w, conv_b,
                    lru_w_a, lru_b_a, lru_w_x, lru_b_x, lru_lambda, w_proj_attn, w_proj_rnn, w_out,
                    ln2_g, ln2_b, ffn2_w_gate, ffn2_w_up, ffn2_w_down, ln3_g, ln3_b)
```

```python
import functools

import jax
import jax.numpy as jnp
import numpy as np
from jax import lax
from jax.experimental import pallas as pl
from jax.experimental.pallas import tpu as pltpu

F32 = jnp.float32
BF16 = jnp.bfloat16
I32 = jnp.int32

N_HEADS = 8
HEAD_DIM = 128
N_IDX_HEADS = 16
IDX_DIM = 64
TOPK_MAX = 256
N_RNN_BLOCKS = 8
RNN_BLOCK = 128
CONV_WIDTH = 4
LRU_C = 8.0
LN_EPS = 1e-5
DEPTH = 1
ALPHA = (2.0 * DEPTH) ** 0.25
NEG_INF = -1e30

VMEM_LIMIT = 56 * 1024 * 1024


def _resident(shape, index_map):
    return pl.BlockSpec(shape, index_map, pipeline_mode=pl.Buffered(1))


def _layer_norm(y, g, b):
    mu = jnp.mean(y, axis=-1, keepdims=True)
    d = y - mu
    var = jnp.mean(d * d, axis=-1, keepdims=True)
    return d * lax.rsqrt(var + LN_EPS) * g + b


def _dot(a, b):
    return jnp.dot(a, b, preferred_element_type=F32)


def _dot_nt(a, b):
    return lax.dot_general(a, b, (((1,), (1,)), ((), ())), preferred_element_type=F32)


FF_CHUNK = 256


def _ffn_ln_kernel(x_ref, wg_ref, wu_ref, wd_ref, g_ref, b_ref, o_ref, acc_ref):
    x = x_ref[...]
    xb = x.astype(BF16)
    d_ff = wg_ref.shape[1]
    for c in range(d_ff // FF_CHUNK):
        sl = slice(c * FF_CHUNK, (c + 1) * FF_CHUNK)
        g = _dot(xb, wg_ref[:, sl])
        u = _dot(xb, wu_ref[:, sl])
        hid = (g * jax.nn.sigmoid(g) * u).astype(BF16)
        part = _dot(hid, wd_ref[sl, :])
        if c == 0:
            acc_ref[...] = part
        else:
            acc_ref[...] += part
    y = ALPHA * x + 0.5 * acc_ref[...]
    o_ref[...] = _layer_norm(y, g_ref[...], b_ref[...])


def _ffn_ln(x, wg, wu, wd, g, b, tm=512):
    n, d = x.shape
    d_ff = wg.shape[1]
    assert n % tm == 0 and d_ff % FF_CHUNK == 0
    return pl.pallas_call(
        _ffn_ln_kernel,
        grid=(n // tm,),
        in_specs=[
            pl.BlockSpec((tm, d), lambda i: (i, 0)),
            _resident((d, d_ff), lambda i: (0, 0)),
            _resident((d, d_ff), lambda i: (0, 0)),
            _resident((d_ff, d), lambda i: (0, 0)),
            _resident((1, d), lambda i: (0, 0)),
            _resident((1, d), lambda i: (0, 0)),
        ],
        out_specs=pl.BlockSpec((tm, d), lambda i: (i, 0)),
        out_shape=jax.ShapeDtypeStruct((n, d), F32),
        scratch_shapes=[pltpu.VMEM((tm, d), F32)],
        compiler_params=pltpu.CompilerParams(
            dimension_semantics=("arbitrary",), vmem_limit_bytes=VMEM_LIMIT),
        name="ffn_ln",
    )(x, wg, wu, wd, g, b)


CONV_HALO = 8
PROJ_PIECE = 512


def _rglru_coeffs(x, cw_ref, cb_ref, wg_ref, ba_ref, bx_ref, lam_ref, xpad_ref, a_ref, b_ref, between):
    tr = x.shape[0]
    xpad_ref[CONV_HALO:, :] = x
    xc = cb_ref[...] + cw_ref[CONV_WIDTH - 1:CONV_WIDTH, :] * x
    for j in range(CONV_WIDTH - 1):
        back = CONV_WIDTH - 1 - j
        xc = xc + cw_ref[j:j + 1, :] * xpad_ref[CONV_HALO - back:CONV_HALO - back + tr, :]
    xpad_ref[0:CONV_HALO, :] = x[tr - CONV_HALO:, :]

    xcb = xc.astype(BF16)
    sp = jax.nn.softplus(-lam_ref[...])
    for n in range(N_RNN_BLOCKS):
        cs = slice(n * RNN_BLOCK, (n + 1) * RNN_BLOCK)
        g2 = _dot(xcb[:, cs], wg_ref[n])
        r = jax.nn.sigmoid(g2[:, :RNN_BLOCK] + ba_ref[:, cs])
        ig = jax.nn.sigmoid(g2[:, RNN_BLOCK:] + bx_ref[:, cs])
        log_a = -LRU_C * r * sp[:, cs]
        a = jnp.exp(log_a)
        one_minus_a2 = jnp.tanh(-log_a) * (a * a + 1.0)
        a_ref[:, cs] = a
        b_ref[:, cs] = jnp.sqrt(one_minus_a2) * (ig * xc[:, cs])
        between[n]()


def _rglru_scan(a_ref, b_ref, h_ref):
    h = h_ref[...]
    row = lax.broadcasted_iota(I32, (8, a_ref.shape[1]), 0)
    for g in range(a_ref.shape[0] // 8):
        a = a_ref[g * 8:(g + 1) * 8, :]
        b = b_ref[g * 8:(g + 1) * 8, :]
        for d in (1, 2, 4):
            prev_a = pltpu.roll(a, d, axis=0)
            prev_b = pltpu.roll(b, d, axis=0)
            b = jnp.where(row >= d, a * prev_b + b, b)
            a = jnp.where(row >= d, a * prev_a, a)
        hs = a * h + b
        a_ref[g * 8:(g + 1) * 8, :] = hs
        h = hs[7:8, :]
    h_ref[...] = h


def _in_proj_kernel(h_ref, wqT, wk, wvT, wqiT, wki, wwiT, wxg, wab,
                    bqT, bk, bvT, bqiT, bki, bwiT, bxg, bab,
                    cw_ref, cb_ref, wg_ref, ba_ref, bx_ref, lam_ref,
                    qT_o, k_o, vT_o, qiT_o, ki_o, wiT_o, gab_o, ornn_o,
                    xpad_ref, a_ref, b_ref, hst_ref, *, tk):
    i = pl.program_id(1)

    @pl.when(i == 0)
    def _():
        xpad_ref[0:CONV_HALO, :] = jnp.zeros((CONV_HALO, xpad_ref.shape[1]), F32)
        hst_ref[...] = jnp.zeros_like(hst_ref)

    hb = h_ref[...].astype(BF16)
    tm = hb.shape[0]
    d_rnn = cw_ref.shape[1]
    xg = _dot(hb, wxg[...]) + bxg[...]

    def rows_piece(w, b_, out, lo):
        def run():
            sl = slice(lo, lo + PROJ_PIECE)
            out[:, sl] = (_dot(hb, w[:, sl]) + b_[:, sl]).astype(out.dtype)
        return run

    def cols_piece(wT, bT, store, lo, scale=1.0):
        def run():
            sl = slice(lo, lo + PROJ_PIECE)
            store(sl, (_dot_nt(wT[sl, :], hb) + bT[sl, :]) * scale)
        return run

    def store_qT(sl, v):
        qT_o[sl, :] = v.astype(BF16)

    def store_qiT(sl, v):
        qiT_o[sl, :] = v.astype(BF16)

    def store_vT(sl, v):
        for c in range(tm // tk):
            vT_o[c, sl, :] = v[:, c * tk:(c + 1) * tk].astype(BF16)

    def small_pieces():
        ki_o[...] = (_dot(hb, wki[...]) + bki[...]).astype(BF16)
        wiT_o[...] = (_dot_nt(wwiT[...], hb) + bwiT[...]) * (N_IDX_HEADS ** -0.5 * IDX_DIM ** -0.5)

    q_scale = HEAD_DIM ** -0.5 * LOG2E
    pieces = [rows_piece(wk, bk, k_o, lo) for lo in range(0, k_o.shape[1], PROJ_PIECE)]
    pieces += [rows_piece(wab, bab, gab_o, lo) for lo in range(0, gab_o.shape[1], PROJ_PIECE)]
    pieces += [cols_piece(wqT, bqT, store_qT, lo, q_scale) for lo in range(0, qT_o.shape[0], PROJ_PIECE)]
    pieces += [cols_piece(wqiT, bqiT, store_qiT, lo) for lo in range(0, qiT_o.shape[0], PROJ_PIECE)]
    pieces += [cols_piece(wvT, bvT, store_vT, lo) for lo in range(0, wvT.shape[0], PROJ_PIECE)]
    pieces += [small_pieces]

    _rglru_coeffs(xg[:, :d_rnn], cw_ref, cb_ref, wg_ref, ba_ref, bx_ref, lam_ref, xpad_ref, a_ref, b_ref,
                  pieces[:N_RNN_BLOCKS])
    _rglru_scan(a_ref, b_ref, hst_ref)
    ornn_o[...] = (a_ref[...] * jax.nn.gelu(xg[:, d_rnn:], approximate=True)).astype(BF16)
    for piece in pieces[N_RNN_BLOCKS:]:
        piece()


def _in_proj(h, ws, bs, rnn, batch, seq, tk, tm=256):
    n, d = h.shape
    nt = seq // tm
    d_attn = N_HEADS * HEAD_DIM
    d_qi = N_IDX_HEADS * IDX_DIM
    d_rnn = rnn[0].shape[1]
    row = lambda b, i: (b * nt + i, 0)
    col = lambda b, i: (b, 0, i)
    const = lambda b, i: (0, 0)
    in_specs = [pl.BlockSpec((tm, d), row)]
    in_specs += [_resident(w.shape, const) for w in ws]
    in_specs += [_resident(b_.shape, const) for b_ in bs]
    in_specs += [_resident(p.shape, (lambda b, i, nd=p.ndim: (0,) * nd)) for p in rnn]
    out_shape = [
        jax.ShapeDtypeStruct((batch, d_attn, seq), BF16),
        jax.ShapeDtypeStruct((n, d_attn), BF16),
        jax.ShapeDtypeStruct((batch, seq // tk, d_attn, tk), BF16),
        jax.ShapeDtypeStruct((batch, d_qi, seq), BF16),
        jax.ShapeDtypeStruct((n, IDX_DIM), BF16),
        jax.ShapeDtypeStruct((batch, N_IDX_HEADS, seq), F32),
        jax.ShapeDtypeStruct((n, ws[7].shape[1]), F32),
        jax.ShapeDtypeStruct((n, d_rnn), BF16),
    ]
    out_specs = [
        pl.BlockSpec((None, d_attn, tm), col),
        pl.BlockSpec((tm, d_attn), row),
        pl.BlockSpec((None, tm // tk, d_attn, tk), lambda b, i: (b, i, 0, 0)),
        pl.BlockSpec((None, d_qi, tm), col),
        pl.BlockSpec((tm, IDX_DIM), row),
        pl.BlockSpec((None, N_IDX_HEADS, tm), col),
        pl.BlockSpec((tm, ws[7].shape[1]), row),
        pl.BlockSpec((tm, d_rnn), row),
    ]
    return pl.pallas_call(
        functools.partial(_in_proj_kernel, tk=tk),
        grid=(batch, nt),
        in_specs=in_specs,
        out_specs=out_specs,
        out_shape=out_shape,
        scratch_shapes=[pltpu.VMEM((tm + CONV_HALO, d_rnn), F32), pltpu.VMEM((tm, d_rnn), F32),
                        pltpu.VMEM((tm, d_rnn), F32), pltpu.VMEM((1, d_rnn), F32)],
        compiler_params=pltpu.CompilerParams(
            dimension_semantics=("arbitrary", "arbitrary"), vmem_limit_bytes=VMEM_LIMIT),
        name="in_proj",
    )(h, *ws, *bs, *rnn)


COUNT_ROWS = 64
MAX_EXTRACT = 3
MIN_SEARCH_ITERS = 10
MAX_SEARCH_ITERS = 400
LOG2E = 1.4426950408889634
POS_RADIX = 64
N_SLOPE_PIECES = 3


def _bf16_pieces(c):
    out = []
    for _ in range(N_SLOPE_PIECES):
        piece = float(np.asarray(c, np.float32).astype(BF16).astype(np.float32))
        out.append(piece)
        c = c - piece
    return out


def _dsa_kernel(qT_ref, qiT_ref, wiT_ref, qiTn_ref, wiTn_ref, ki_ref, k_ref, vT_ref, o_ref,
                sc_ref, gmax_ref, l_ref, acc_ref, qaug_ref, s_ref, pos_ref, *, tq, tk, topk):
    i = pl.program_id(1)
    nkb = i + 1
    q_pos = i * tq + lax.broadcasted_iota(I32, (tk, tq), 1)
    key_row = lax.broadcasted_iota(I32, (tk, tq), 0)
    q_row = i * tq + lax.broadcasted_iota(I32, (1, tq), 1)

    def score_block(j, qi_ref, wi_ref, diag_first_query):
        k0 = pl.multiple_of(j * tk, tk)
        ki = ki_ref[pl.ds(k0, tk), :]
        acc = jnp.zeros((tk, tq), F32)
        for h in range(N_IDX_HEADS):
            r = _dot(ki, qi_ref[h * IDX_DIM:(h + 1) * IDX_DIM, :])
            acc = acc + wi_ref[h:h + 1, :] * jnp.maximum(r, 0.0)
        if diag_first_query is not None:
            acc = jnp.where(k0 + key_row <= diag_first_query + (q_pos - i * tq), acc, NEG_INF)
        sc_ref[j] = acc
        gmax_ref[...] = jnp.maximum(gmax_ref[...], acc)

    def score_next(j):
        score_block(j, qiTn_ref, wiTn_ref, None)

    @pl.when(i == 0)
    def _():
        gmax_ref[...] = jnp.full(gmax_ref.shape, -jnp.inf, F32)
        score_block(0, qiT_ref, wiT_ref, 0)

    n_pairs_all = (nkb + 1) // 2

    @pl.when(nkb % 2 == 1)
    def _():
        sc_ref[nkb] = jnp.full((tk, tq), NEG_INF, F32)

    def count_where(pred):
        def body(p, acc):
            for j in (2 * p, 2 * p + 1):
                m = jnp.where(pred(sc_ref[j], j), 1, 0).astype(I32)
                for r in range(tk // COUNT_ROWS):
                    acc = acc + m[r * COUNT_ROWS:(r + 1) * COUNT_ROWS, :]
            return acc
        acc = lax.fori_loop(0, n_pairs_all, body, jnp.zeros((COUNT_ROWS, tq), I32))
        return jnp.sum(acc, axis=0, keepdims=True)

    def count_ge(v):
        return count_where(lambda x, j: x >= v)

    gmax = gmax_ref[...]
    lo = jnp.min(gmax, axis=0, keepdims=True)
    ub = jnp.max(gmax, axis=0, keepdims=True)
    hi = ub + jnp.maximum(jnp.abs(ub) * 2.0 ** -20, 1e-30)
    all_rows = q_row + 1 <= topk
    lo = jnp.where(all_rows, NEG_INF, lo)
    c_lo = jnp.where(all_rows, topk, count_ge(lo))
    c_hi = jnp.zeros((1, tq), I32)
    done = (c_lo == topk).astype(I32)

    def search_step(st):
        it, lo, hi, c_lo, c_hi, done = st
        v = 0.5 * lo + 0.5 * hi
        adjacent = jnp.logical_or(v <= lo, v >= hi)
        c = count_ge(v)
        live = jnp.logical_and(done == 0, jnp.logical_not(adjacent))
        up = jnp.logical_and(live, c >= topk)
        dn = jnp.logical_and(live, c < topk)
        lo = jnp.where(up, v, lo)
        c_lo = jnp.where(up, c, c_lo)
        hi = jnp.where(dn, v, hi)
        c_hi = jnp.where(dn, c, c_hi)
        done = jnp.where(jnp.logical_or(adjacent, c_lo == topk), 1, done)
        return it + 1, lo, hi, c_lo, c_hi, done

    def all_done(st):
        return jnp.min(st[5]) == 1

    def near_top(st):
        it, lo, hi, c_lo, c_hi, done = st
        return jnp.min(jnp.where(jnp.logical_or(done == 1, topk - c_hi <= MAX_EXTRACT), 1, 0)) == 1

    def keep_bisecting(stop):
        return lambda st: jnp.logical_and(st[0] < MAX_SEARCH_ITERS, jnp.logical_not(stop(st)))

    st = (jnp.int32(0), lo, hi, c_lo, c_hi, done)
    st = lax.fori_loop(0, MIN_SEARCH_ITERS, lambda t, st: search_step(st), st)
    st = lax.while_loop(keep_bisecting(near_top), search_step, st)
    it, lo, hi, c_lo, c_hi, done = st

    def step_down(t, sd):
        hi_s, c_s = sd
        def body(p, acc):
            for j in (2 * p, 2 * p + 1):
                x = jnp.where(sc_ref[j] < hi_s, sc_ref[j], -jnp.inf)
                for r in range(tk // COUNT_ROWS):
                    acc = jnp.maximum(acc, x[r * COUNT_ROWS:(r + 1) * COUNT_ROWS, :])
            return acc
        acc = lax.fori_loop(0, n_pairs_all, body, jnp.full((COUNT_ROWS, tq), -jnp.inf, F32))
        below = jnp.max(acc, axis=0, keepdims=True)
        move = jnp.logical_and(done == 0, c_s < topk)
        return jnp.where(move, below, hi_s), jnp.where(move, c_s + 1, c_s)

    hi_s, c_s = lax.fori_loop(0, MAX_EXTRACT, step_down, (hi, c_hi))
    stepped = jnp.logical_and(done == 0, c_s == topk)
    c_chk = count_ge(jnp.where(stepped, hi_s, lo))
    lo = jnp.where(stepped, hi_s, lo)
    c_lo = jnp.where(stepped, c_chk, c_lo)
    done = jnp.where(c_lo == topk, 1, done)
    _, thr, _, n_ge, _, _ = lax.while_loop(keep_bisecting(all_done), search_step,
                                           (it, lo, hi, c_lo, c_hi, done))

    @pl.when(jnp.max(n_ge) > topk)
    def _():
        need = topk - count_where(lambda x, j: x > thr)

        idx_bits = (sc_ref.shape[0] * tk - 1).bit_length()

        def idx_step(t, c):
            cand = c | jnp.left_shift(jnp.int32(1), idx_bits - 1 - t)
            below = count_where(lambda x, j: jnp.logical_and(x == thr, j * tk + key_row < cand))
            return jnp.where(below < need, cand, c)

        cut = lax.fori_loop(0, idx_bits, idx_step, jnp.zeros((1, tq), I32))

        def demote_block(j, carry):
            x = sc_ref[j]
            drop = jnp.logical_and(x == thr, j * tk + key_row > cut)
            sc_ref[j] = jnp.where(drop, NEG_INF, x)
            return carry

        lax.fori_loop(0, nkb, demote_block, 0)

    slope_pieces = [_bf16_pieces(2.0 ** (-8.0 * (h + 1) / N_HEADS) * LOG2E) for h in range(N_HEADS)]
    slope_l2 = [sum(pieces) for pieces in slope_pieces]
    row_a = lax.broadcasted_iota(I32, (HEAD_DIM, tq), 0)
    for h in range(N_HEADS):
        hs = slice(h * HEAD_DIM, (h + 1) * HEAD_DIM)
        aug = jnp.zeros((HEAD_DIM, tq), F32)
        for r, c in enumerate(slope_pieces[h]):
            aug = jnp.where(row_a == 2 * r, c * POS_RADIX, jnp.where(row_a == 2 * r + 1, c, aug))
        qaug_ref[h, 0:HEAD_DIM, :] = qT_ref[hs, :]
        qaug_ref[h, HEAD_DIM:, :] = aug.astype(BF16)
    lane_p = lax.broadcasted_iota(I32, (tk, HEAD_DIM), 1)
    row_p = lax.broadcasted_iota(I32, (tk, HEAD_DIM), 0)
    pos = jnp.where(lane_p % 2 == 0, row_p // POS_RADIX, row_p % POS_RADIX)
    pos_ref[...] = jnp.where(lane_p < 2 * N_SLOPE_PIECES, pos, 0).astype(F32).astype(BF16)

    l_ref[...] = jnp.zeros(l_ref.shape, F32)
    acc_ref[...] = jnp.zeros(acc_ref.shape, F32)

    key_minus_query = key_row - (q_pos - i * tq)

    def logits_block(j, scores, m_run, slot):
        k0 = pl.multiple_of(j * tk, tk)
        k0f = jnp.asarray(j * tk).astype(F32)
        selected = jnp.logical_and(scores >= thr, key_minus_query <= i * tq - j * tk)
        bias = jnp.where(selected, 0.0, NEG_INF)
        m_blk = []
        for h in range(N_HEADS):
            hs = slice(h * HEAD_DIM, (h + 1) * HEAD_DIM)
            lhs = jnp.concatenate([k_ref[pl.ds(k0, tk), hs], pos_ref[...]], axis=1)
            s = _dot(lhs, qaug_ref[h]) + bias
            s_ref[slot, h] = s
            m_blk.append(jnp.max(s, axis=0, keepdims=True) + k0f * slope_l2[h])
        m_new = jnp.maximum(m_run, jnp.concatenate(m_blk, axis=0))
        return m_new, jnp.exp2(m_run - m_new)

    def values_block(j, m_j, alpha, slot):
        sums = []
        for h in range(N_HEADS):
            hs = slice(h * HEAD_DIM, (h + 1) * HEAD_DIM)
            m_local = m_j[h:h + 1, :] - jnp.asarray(j * tk).astype(F32) * slope_l2[h]
            p = jnp.exp2(s_ref[slot, h] - m_local)
            sums.append(jnp.sum(p, axis=0, keepdims=True))
            pv = _dot(vT_ref[j, hs, :], p.astype(BF16))
            acc_ref[hs, :] = alpha[h:h + 1, :] * acc_ref[hs, :] + pv
        l_ref[...] = alpha * l_ref[...] + jnp.concatenate(sums, axis=0)

    def attn_step(j, carry, slot, scores_next):
        m_j, alpha_j = carry
        nxt = logits_block(j + 1, scores_next, m_j, 1 - slot)
        values_block(j, m_j, alpha_j, slot)
        return nxt

    def attn_pair(p, carry):
        scores_a = sc_ref[2 * p + 1]
        scores_b = sc_ref[2 * p + 2]
        carry = attn_step(2 * p, carry, 0, scores_a)
        score_next(2 * p)
        carry = attn_step(2 * p + 1, carry, 1, scores_b)
        score_next(2 * p + 1)
        return carry

    gmax_ref[...] = jnp.full(gmax_ref.shape, -jnp.inf, F32)
    first = logits_block(0, sc_ref[0], jnp.full((N_HEADS, tq), -jnp.inf, F32), 0)
    n_pairs = (nkb - 1) // 2
    m_c, alpha_c = lax.fori_loop(0, n_pairs, attn_pair, first)
    j_rest = 2 * n_pairs

    @pl.when(j_rest == nkb - 1)
    def _():
        values_block(j_rest, m_c, alpha_c, 0)
        score_next(j_rest)

    @pl.when(j_rest < nkb - 1)
    def _():
        m_l, alpha_l = attn_step(j_rest, (m_c, alpha_c), 0, sc_ref[j_rest + 1])
        values_block(j_rest + 1, m_l, alpha_l, 1)
        score_next(j_rest)
        score_next(j_rest + 1)

    @pl.when(i + 1 < pl.num_programs(1))
    def _():
        score_block(nkb, qiTn_ref, wiTn_ref, (i + 1) * tq)

    inv_l = 1.0 / l_ref[...]
    for h in range(N_HEADS):
        hs = slice(h * HEAD_DIM, (h + 1) * HEAD_DIM)
        o_ref[:, hs] = (acc_ref[hs, :] * inv_l[h:h + 1, :]).T.astype(BF16)


def _dsa(qT, qiT, wiT, ki, k, vT, batch, seq, tq, tk):
    d_attn = qT.shape[1]
    nq = seq // tq
    topk = min(TOPK_MAX, seq // 4)
    assert tk >= topk and tk % COUNT_ROWS == 0 and seq % tk == 0 and tq == tk
    col = lambda b, i: (b, 0, i)
    col_next = lambda b, i: (b, 0, jnp.minimum(i + 1, nq - 1))
    kernel = functools.partial(_dsa_kernel, tq=tq, tk=tk, topk=topk)
    return pl.pallas_call(
        kernel,
        grid=(batch, nq),
        in_specs=[
            pl.BlockSpec((None, d_attn, tq), col),
            pl.BlockSpec((None, qiT.shape[1], tq), col),
            pl.BlockSpec((None, N_IDX_HEADS, tq), col),
            pl.BlockSpec((None, qiT.shape[1], tq), col_next),
            pl.BlockSpec((None, N_IDX_HEADS, tq), col_next),
            _resident((seq, IDX_DIM), lambda b, i: (b, 0)),
            _resident((seq, d_attn), lambda b, i: (b, 0)),
            _resident((None, seq // tk, d_attn, tk), lambda b, i: (b, 0, 0, 0)),
        ],
        out_specs=pl.BlockSpec((tq, d_attn), lambda b, i: (b * nq + i, 0)),
        out_shape=jax.ShapeDtypeStruct((batch * seq, d_attn), BF16),
        scratch_shapes=[
            pltpu.VMEM((seq // tk, tk, tq), F32),
            pltpu.VMEM((tk, tq), F32),
            pltpu.VMEM((N_HEADS, tq), F32),
            pltpu.VMEM((d_attn, tq), F32),
            pltpu.VMEM((N_HEADS, 2 * HEAD_DIM, tq), BF16),
            pltpu.VMEM((2, N_HEADS, tk, tq), F32),
            pltpu.VMEM((tk, HEAD_DIM), BF16),
        ],
        compiler_params=pltpu.CompilerParams(
            dimension_semantics=("arbitrary", "arbitrary"), vmem_limit_bytes=VMEM_LIMIT),
        name="dsa",
    )(qT, qiT, wiT, qiT, wiT, ki, k, vT)


def _merge_ln_kernel(h_ref, oa_ref, or_ref, ga_ref, gb_ref, wpa, wpr, wout, g_ref, b_ref, o_ref):
    ya = _dot(oa_ref[...], wpa[...])
    yr = _dot(or_ref[...], wpr[...])
    y = jax.nn.sigmoid(ga_ref[...]) * ya + jax.nn.sigmoid(gb_ref[...]) * yr
    mix = _dot(y.astype(BF16), wout[...])
    o_ref[...] = _layer_norm(ALPHA * h_ref[...] + mix, g_ref[...], b_ref[...])


def _merge_ln(h, o_attn, o_rnn, gab, wpa, wpr, wout, g, b, tm=512):
    n, d = h.shape
    const = lambda i: (0, 0)
    return pl.pallas_call(
        _merge_ln_kernel,
        grid=(n // tm,),
        in_specs=[
            pl.BlockSpec((tm, d), lambda i: (i, 0)),
            pl.BlockSpec((tm, o_attn.shape[1]), lambda i: (i, 0)),
            pl.BlockSpec((tm, o_rnn.shape[1]), lambda i: (i, 0)),
            pl.BlockSpec((tm, d), lambda i: (i, 0)),
            pl.BlockSpec((tm, d), lambda i: (i, 1)),
            _resident(wpa.shape, const),
            _resident(wpr.shape, const),
            _resident(wout.shape, const),
            _resident((1, d), const),
            _resident((1, d), const),
        ],
        out_specs=pl.BlockSpec((tm, d), lambda i: (i, 0)),
        out_shape=jax.ShapeDtypeStruct((n, d), F32),
        compiler_params=pltpu.CompilerParams(
            dimension_semantics=("arbitrary",), vmem_limit_bytes=VMEM_LIMIT),
        name="merge_ln",
    )(h, o_attn, o_rnn, gab, gab, wpa, wpr, wout, g, b)


def _layer(x2, batch, seq, p, tq=256, tk=256):
    d_attn = N_HEADS * HEAD_DIM
    row = lambda a: a.reshape(1, -1).astype(F32)
    colv = lambda a: a.reshape(-1, 1).astype(F32)

    h1 = _ffn_ln(x2, p['ffn1_w_gate'].astype(BF16), p['ffn1_w_up'].astype(BF16),
                 p['ffn1_w_down'].astype(BF16), row(p['ln1_g']), row(p['ln1_b']))

    w_in, b_in = p['w_in'], p['b_in']
    o = 0
    wq, bq = w_in[:, o:o + d_attn], b_in[o:o + d_attn]; o += d_attn
    wk, bk = w_in[:, o:o + d_attn], b_in[o:o + d_attn]; o += d_attn
    wv, bv = w_in[:, o:o + d_attn], b_in[o:o + d_attn]; o += d_attn
    n_qi = N_IDX_HEADS * IDX_DIM
    wqi, bqi = w_in[:, o:o + n_qi], b_in[o:o + n_qi]; o += n_qi
    wki, bki = w_in[:, o:o + IDX_DIM], b_in[o:o + IDX_DIM]; o += IDX_DIM
    wwi, bwi = w_in[:, o:o + N_IDX_HEADS], b_in[o:o + N_IDX_HEADS]; o += N_IDX_HEADS
    d_rnn = N_RNN_BLOCKS * RNN_BLOCK
    wxg, bxg = w_in[:, o:o + 2 * d_rnn], b_in[o:o + 2 * d_rnn]; o += 2 * d_rnn
    wab, bab = w_in[:, o:], b_in[o:]
    ws = [w.astype(BF16) for w in (wq.T, wk, wv.T, wqi.T, wki, wwi.T, wxg, wab)]
    bs = [colv(bq), row(bk), colv(bv), colv(bqi), row(bki), colv(bwi), row(bxg), row(bab)]
    w_gates = jnp.concatenate([p['lru_w_a'], p['lru_w_x']], axis=-1).astype(BF16)
    rnn = [p['conv_w'].astype(F32), row(p['conv_b']), w_gates, row(p['lru_b_a']), row(p['lru_b_x']),
           row(p['lru_lambda'])]
    qT, k, vT, qiT, ki, wiT, gab, o_rnn = _in_proj(h1, ws, bs, rnn, batch, seq, tk)

    o_attn = _dsa(qT, qiT, wiT, ki, k, vT, batch, seq, tq, tk)

    h2 = _merge_ln(h1, o_attn, o_rnn, gab, p['w_proj_attn'].astype(BF16),
                   p['w_proj_rnn'].astype(BF16), p['w_out'].astype(BF16),
                   row(p['ln2_g']), row(p['ln2_b']))

    return _ffn_ln(h2, p['ffn2_w_gate'].astype(BF16), p['ffn2_w_up'].astype(BF16),
                   p['ffn2_w_down'].astype(BF16), row(p['ln3_g']), row(p['ln3_b']))


_PARAM_NAMES = ('ln1_g', 'ln1_b', 'ffn1_w_gate', 'ffn1_w_up', 'ffn1_w_down', 'w_in', 'b_in', 'conv_w',
                'conv_b', 'lru_w_a', 'lru_b_a', 'lru_w_x', 'lru_b_x', 'lru_lambda', 'w_proj_attn',
                'w_proj_rnn', 'w_out', 'ln2_g', 'ln2_b', 'ffn2_w_gate', 'ffn2_w_up', 'ffn2_w_down',
                'ln3_g', 'ln3_b')


@jax.jit
def _forward(x, *params):
    batch, seq, d = x.shape
    h = x.reshape(batch * seq, d)
    for l in range(DEPTH):
        p = {name: a[l] for name, a in zip(_PARAM_NAMES, params)}
        h = _layer(h, batch, seq, p)
    return h.reshape(batch, seq, d)


def kernel(x, ln1_g, ln1_b, ffn1_w_gate, ffn1_w_up, ffn1_w_down, w_in, b_in, conv_w, conv_b, lru_w_a, lru_b_a, lru_w_x, lru_b_x, lru_lambda, w_proj_attn, w_proj_rnn, w_out, ln2_g, ln2_b, ffn2_w_gate, ffn2_w_up, ffn2_w_down, ln3_g, ln3_b):
    return _forward(x, ln1_g, ln1_b, ffn1_w_gate, ffn1_w_up, ffn1_w_down, w_in, b_in, conv_w, conv_b,
                    lru_w_a, lru_b_a, lru_w_x, lru_b_x, lru_lambda, w_proj_attn, w_proj_rnn, w_out,
                    ln2_g, ln2_b, ffn2_w_gate, ffn2_w_up, ffn2_w_down, ln3_g, ln3_b)
```

```python
import functools

import jax
import jax.numpy as jnp
import numpy as np
from jax import lax
from jax.experimental import pallas as pl
from jax.experimental.pallas import tpu as pltpu

F32 = jnp.float32
BF16 = jnp.bfloat16
I32 = jnp.int32

N_HEADS = 8
HEAD_DIM = 128
N_IDX_HEADS = 16
IDX_DIM = 64
TOPK_MAX = 256
N_RNN_BLOCKS = 8
RNN_BLOCK = 128
CONV_WIDTH = 4
LRU_C = 8.0
LN_EPS = 1e-5
DEPTH = 1
ALPHA = (2.0 * DEPTH) ** 0.25
NEG_INF = -1e30

VMEM_LIMIT = 56 * 1024 * 1024


def _resident(shape, index_map):
    return pl.BlockSpec(shape, index_map, pipeline_mode=pl.Buffered(1))


def _layer_norm(y, g, b):
    mu = jnp.mean(y, axis=-1, keepdims=True)
    d = y - mu
    var = jnp.mean(d * d, axis=-1, keepdims=True)
    return d * lax.rsqrt(var + LN_EPS) * g + b


def _dot(a, b):
    return jnp.dot(a, b, preferred_element_type=F32)


def _dot_nt(a, b):
    return lax.dot_general(a, b, (((1,), (1,)), ((), ())), preferred_element_type=F32)


FF_CHUNK = 256


def _ffn_ln_kernel(x_ref, wg_ref, wu_ref, wd_ref, g_ref, b_ref, o_ref, acc_ref):
    x = x_ref[...]
    xb = x.astype(BF16)
    d_ff = wg_ref.shape[1]
    for c in range(d_ff // FF_CHUNK):
        sl = slice(c * FF_CHUNK, (c + 1) * FF_CHUNK)
        g = _dot(xb, wg_ref[:, sl])
        u = _dot(xb, wu_ref[:, sl])
        hid = (g * jax.nn.sigmoid(g) * u).astype(BF16)
        part = _dot(hid, wd_ref[sl, :])
        if c == 0:
            acc_ref[...] = part
        else:
            acc_ref[...] += part
    y = ALPHA * x + 0.5 * acc_ref[...]
    o_ref[...] = _layer_norm(y, g_ref[...], b_ref[...])


def _ffn_ln(x, wg, wu, wd, g, b, tm=512):
    n, d = x.shape
    d_ff = wg.shape[1]
    assert n % tm == 0 and d_ff % FF_CHUNK == 0
    return pl.pallas_call(
        _ffn_ln_kernel,
        grid=(n // tm,),
        in_specs=[
            pl.BlockSpec((tm, d), lambda i: (i, 0)),
            _resident((d, d_ff), lambda i: (0, 0)),
            _resident((d, d_ff), lambda i: (0, 0)),
            _resident((d_ff, d), lambda i: (0, 0)),
            _resident((1, d), lambda i: (0, 0)),
            _resident((1, d), lambda i: (0, 0)),
        ],
        out_specs=pl.BlockSpec((tm, d), lambda i: (i, 0)),
        out_shape=jax.ShapeDtypeStruct((n, d), F32),
        scratch_shapes=[pltpu.VMEM((tm, d), F32)],
        compiler_params=pltpu.CompilerParams(
            dimension_semantics=("arbitrary",), vmem_limit_bytes=VMEM_LIMIT),
        name="ffn_ln",
    )(x, wg, wu, wd, g, b)


CONV_HALO = 8
PROJ_PIECE = 256
PIECES_PER_STAGE = 2


def _rglru_coeffs(x, cw_ref, cb_ref, wg_ref, ba_ref, bx_ref, lam_ref, xpad_ref, a_ref, b_ref, between):
    tr = x.shape[0]
    xpad_ref[CONV_HALO:, :] = x
    xc = cb_ref[...] + cw_ref[CONV_WIDTH - 1:CONV_WIDTH, :] * x
    for j in range(CONV_WIDTH - 1):
        back = CONV_WIDTH - 1 - j
        xc = xc + cw_ref[j:j + 1, :] * xpad_ref[CONV_HALO - back:CONV_HALO - back + tr, :]
    xpad_ref[0:CONV_HALO, :] = x[tr - CONV_HALO:, :]

    xcb = xc.astype(BF16)
    sp = jax.nn.softplus(-lam_ref[...])
    for n in range(N_RNN_BLOCKS):
        cs = slice(n * RNN_BLOCK, (n + 1) * RNN_BLOCK)
        g2 = _dot(xcb[:, cs], wg_ref[n])
        r = jax.nn.sigmoid(g2[:, :RNN_BLOCK] + ba_ref[:, cs])
        ig = jax.nn.sigmoid(g2[:, RNN_BLOCK:] + bx_ref[:, cs])
        log_a = -LRU_C * r * sp[:, cs]
        a = jnp.exp(log_a)
        one_minus_a2 = jnp.tanh(-log_a) * (a * a + 1.0)
        a_ref[:, cs] = a
        b_ref[:, cs] = jnp.sqrt(one_minus_a2) * (ig * xc[:, cs])
        between[n]()


def _rglru_scan(a_ref, b_ref, h_ref):
    h = h_ref[...]
    row = lax.broadcasted_iota(I32, (8, a_ref.shape[1]), 0)
    for g in range(a_ref.shape[0] // 8):
        a = a_ref[g * 8:(g + 1) * 8, :]
        b = b_ref[g * 8:(g + 1) * 8, :]
        for d in (1, 2, 4):
            prev_a = pltpu.roll(a, d, axis=0)
            prev_b = pltpu.roll(b, d, axis=0)
            b = jnp.where(row >= d, a * prev_b + b, b)
            a = jnp.where(row >= d, a * prev_a, a)
        hs = a * h + b
        a_ref[g * 8:(g + 1) * 8, :] = hs
        h = hs[7:8, :]
    h_ref[...] = h


def _in_proj_kernel(h_ref, wqT, wk, wvT, wqiT, wki, wwiT, wxg, wab,
                    bqT, bk, bvT, bqiT, bki, bwiT, bxg, bab,
                    cw_ref, cb_ref, wg_ref, ba_ref, bx_ref, lam_ref,
                    qT_o, k_o, vT_o, qiT_o, ki_o, wiT_o, gab_o, ornn_o,
                    xpad_ref, a_ref, b_ref, hst_ref, *, tk):
    i = pl.program_id(1)

    @pl.when(i == 0)
    def _():
        xpad_ref[0:CONV_HALO, :] = jnp.zeros((CONV_HALO, xpad_ref.shape[1]), F32)
        hst_ref[...] = jnp.zeros_like(hst_ref)

    hb = h_ref[...].astype(BF16)
    tm = hb.shape[0]
    d_rnn = cw_ref.shape[1]
    xg = _dot(hb, wxg[...]) + bxg[...]

    def rows_piece(w, b_, out, lo):
        def run():
            sl = slice(lo, lo + PROJ_PIECE)
            out[:, sl] = (_dot(hb, w[:, sl]) + b_[:, sl]).astype(out.dtype)
        return run

    def cols_piece(wT, bT, store, lo, scale=1.0):
        def run():
            sl = slice(lo, lo + PROJ_PIECE)
            store(sl, (_dot_nt(wT[sl, :], hb) + bT[sl, :]) * scale)
        return run

    def store_qT(sl, v):
        qT_o[sl, :] = v.astype(BF16)

    def store_qiT(sl, v):
        qiT_o[sl, :] = v.astype(BF16)

    def store_vT(sl, v):
        for c in range(tm // tk):
            vT_o[c, sl, :] = v[:, c * tk:(c + 1) * tk].astype(BF16)

    def small_pieces():
        ki_o[...] = (_dot(hb, wki[...]) + bki[...]).astype(BF16)
        wiT_o[...] = (_dot_nt(wwiT[...], hb) + bwiT[...]) * (N_IDX_HEADS ** -0.5 * IDX_DIM ** -0.5)

    q_scale = HEAD_DIM ** -0.5 * LOG2E
    pieces = [rows_piece(wk, bk, k_o, lo) for lo in range(0, k_o.shape[1], PROJ_PIECE)]
    pieces += [rows_piece(wab, bab, gab_o, lo) for lo in range(0, gab_o.shape[1], PROJ_PIECE)]
    pieces += [cols_piece(wqT, bqT, store_qT, lo, q_scale) for lo in range(0, qT_o.shape[0], PROJ_PIECE)]
    pieces += [cols_piece(wqiT, bqiT, store_qiT, lo) for lo in range(0, qiT_o.shape[0], PROJ_PIECE)]
    pieces += [cols_piece(wvT, bvT, store_vT, lo) for lo in range(0, wvT.shape[0], PROJ_PIECE)]
    pieces += [small_pieces]

    def run_pieces(lo, hi):
        def run():
            for piece in pieces[lo:hi]:
                piece()
        return run

    _rglru_coeffs(xg[:, :d_rnn], cw_ref, cb_ref, wg_ref, ba_ref, bx_ref, lam_ref, xpad_ref, a_ref, b_ref,
                  [run_pieces(n * PIECES_PER_STAGE, (n + 1) * PIECES_PER_STAGE) for n in range(N_RNN_BLOCKS)])
    _rglru_scan(a_ref, b_ref, hst_ref)
    ornn_o[...] = (a_ref[...] * jax.nn.gelu(xg[:, d_rnn:], approximate=True)).astype(BF16)
    run_pieces(N_RNN_BLOCKS * PIECES_PER_STAGE, len(pieces))()


def _in_proj(h, ws, bs, rnn, batch, seq, tk, tm=256):
    n, d = h.shape
    nt = seq // tm
    d_attn = N_HEADS * HEAD_DIM
    d_qi = N_IDX_HEADS * IDX_DIM
    d_rnn = rnn[0].shape[1]
    row = lambda b, i: (b * nt + i, 0)
    col = lambda b, i: (b, 0, i)
    const = lambda b, i: (0, 0)
    in_specs = [pl.BlockSpec((tm, d), row)]
    in_specs += [_resident(w.shape, const) for w in ws]
    in_specs += [_resident(b_.shape, const) for b_ in bs]
    in_specs += [_resident(p.shape, (lambda b, i, nd=p.ndim: (0,) * nd)) for p in rnn]
    out_shape = [
        jax.ShapeDtypeStruct((batch, d_attn, seq), BF16),
        jax.ShapeDtypeStruct((n, d_attn), BF16),
        jax.ShapeDtypeStruct((batch, seq // tk, d_attn, tk), BF16),
        jax.ShapeDtypeStruct((batch, d_qi, seq), BF16),
        jax.ShapeDtypeStruct((n, IDX_DIM), BF16),
        jax.ShapeDtypeStruct((batch, N_IDX_HEADS, seq), F32),
        jax.ShapeDtypeStruct((n, ws[7].shape[1]), F32),
        jax.ShapeDtypeStruct((n, d_rnn), BF16),
    ]
    out_specs = [
        pl.BlockSpec((None, d_attn, tm), col),
        pl.BlockSpec((tm, d_attn), row),
        pl.BlockSpec((None, tm // tk, d_attn, tk), lambda b, i: (b, i, 0, 0)),
        pl.BlockSpec((None, d_qi, tm), col),
        pl.BlockSpec((tm, IDX_DIM), row),
        pl.BlockSpec((None, N_IDX_HEADS, tm), col),
        pl.BlockSpec((tm, ws[7].shape[1]), row),
        pl.BlockSpec((tm, d_rnn), row),
    ]
    return pl.pallas_call(
        functools.partial(_in_proj_kernel, tk=tk),
        grid=(batch, nt),
        in_specs=in_specs,
        out_specs=out_specs,
        out_shape=out_shape,
        scratch_shapes=[pltpu.VMEM((tm + CONV_HALO, d_rnn), F32), pltpu.VMEM((tm, d_rnn), F32),
                        pltpu.VMEM((tm, d_rnn), F32), pltpu.VMEM((1, d_rnn), F32)],
        compiler_params=pltpu.CompilerParams(
            dimension_semantics=("arbitrary", "arbitrary"), vmem_limit_bytes=VMEM_LIMIT),
        name="in_proj",
    )(h, *ws, *bs, *rnn)


COUNT_ROWS = 64
MAX_EXTRACT = 3
MIN_SEARCH_ITERS = 10
MAX_SEARCH_ITERS = 400
LOG2E = 1.4426950408889634
POS_RADIX = 64
N_SLOPE_PIECES = 3


def _bf16_pieces(c):
    out = []
    for _ in range(N_SLOPE_PIECES):
        piece = float(np.asarray(c, np.float32).astype(BF16).astype(np.float32))
        out.append(piece)
        c = c - piece
    return out


def _dsa_kernel(qT_ref, qiT_ref, wiT_ref, qiTn_ref, wiTn_ref, ki_ref, k_ref, vT_ref, o_ref,
                sc_ref, gmax_ref, l_ref, acc_ref, qaug_ref, s_ref, pos_ref, *, tq, tk, topk):
    i = pl.program_id(1)
    nkb = i + 1
    q_pos = i * tq + lax.broadcasted_iota(I32, (tk, tq), 1)
    key_row = lax.broadcasted_iota(I32, (tk, tq), 0)
    q_row = i * tq + lax.broadcasted_iota(I32, (1, tq), 1)

    def score_block(j, qi_ref, wi_ref, diag_first_query):
        k0 = pl.multiple_of(j * tk, tk)
        ki = ki_ref[pl.ds(k0, tk), :]
        acc = jnp.zeros((tk, tq), F32)
        for h in range(N_IDX_HEADS):
            r = _dot(ki, qi_ref[h * IDX_DIM:(h + 1) * IDX_DIM, :])
            acc = acc + wi_ref[h:h + 1, :] * jnp.maximum(r, 0.0)
        if diag_first_query is not None:
            acc = jnp.where(k0 + key_row <= diag_first_query + (q_pos - i * tq), acc, NEG_INF)
        sc_ref[j] = acc
        gmax_ref[...] = jnp.maximum(gmax_ref[...], acc)

    def score_next(j):
        score_block(j, qiTn_ref, wiTn_ref, None)

    @pl.when(i == 0)
    def _():
        gmax_ref[...] = jnp.full(gmax_ref.shape, -jnp.inf, F32)
        score_block(0, qiT_ref, wiT_ref, 0)

    n_pairs_all = (nkb + 1) // 2

    @pl.when(nkb % 2 == 1)
    def _():
        sc_ref[nkb] = jnp.full((tk, tq), NEG_INF, F32)

    def count_where(pred):
        def body(p, acc):
            for j in (2 * p, 2 * p + 1):
                m = jnp.where(pred(sc_ref[j], j), 1, 0).astype(I32)
                for r in range(tk // COUNT_ROWS):
                    acc = acc + m[r * COUNT_ROWS:(r + 1) * COUNT_ROWS, :]
            return acc
        acc = lax.fori_loop(0, n_pairs_all, body, jnp.zeros((COUNT_ROWS, tq), I32))
        return jnp.sum(acc, axis=0, keepdims=True)

    def count_ge(v):
        return count_where(lambda x, j: x >= v)

    gmax = gmax_ref[...]
    lo = jnp.min(gmax, axis=0, keepdims=True)
    ub = jnp.max(gmax, axis=0, keepdims=True)
    hi = ub + jnp.maximum(jnp.abs(ub) * 2.0 ** -20, 1e-30)
    all_rows = q_row + 1 <= topk
    lo = jnp.where(all_rows, NEG_INF, lo)
    c_lo = jnp.where(all_rows, topk, count_ge(lo))
    c_hi = jnp.zeros((1, tq), I32)
    done = (c_lo == topk).astype(I32)

    def search_step(st):
        it, lo, hi, c_lo, c_hi, done = st
        v = 0.5 * lo + 0.5 * hi
        adjacent = jnp.logical_or(v <= lo, v >= hi)
        c = count_ge(v)
        live = jnp.logical_and(done == 0, jnp.logical_not(adjacent))
        up = jnp.logical_and(live, c >= topk)
        dn = jnp.logical_and(live, c < topk)
        lo = jnp.where(up, v, lo)
        c_lo = jnp.where(up, c, c_lo)
        hi = jnp.where(dn, v, hi)
        c_hi = jnp.where(dn, c, c_hi)
        done = jnp.where(jnp.logical_or(adjacent, c_lo == topk), 1, done)
        return it + 1, lo, hi, c_lo, c_hi, done

    def all_done(st):
        return jnp.min(st[5]) == 1

    def near_top(st):
        it, lo, hi, c_lo, c_hi, done = st
        return jnp.min(jnp.where(jnp.logical_or(done == 1, topk - c_hi <= MAX_EXTRACT), 1, 0)) == 1

    def keep_bisecting(stop):
        return lambda st: jnp.logical_and(st[0] < MAX_SEARCH_ITERS, jnp.logical_not(stop(st)))

    st = (jnp.int32(0), lo, hi, c_lo, c_hi, done)
    st = lax.fori_loop(0, MIN_SEARCH_ITERS, lambda t, st: search_step(st), st)
    st = lax.while_loop(keep_bisecting(near_top), search_step, st)
    it, lo, hi, c_lo, c_hi, done = st

    def step_down(t, sd):
        hi_s, c_s = sd
        def body(p, acc):
            for j in (2 * p, 2 * p + 1):
                x = jnp.where(sc_ref[j] < hi_s, sc_ref[j], -jnp.inf)
                for r in range(tk // COUNT_ROWS):
                    acc = jnp.maximum(acc, x[r * COUNT_ROWS:(r + 1) * COUNT_ROWS, :])
            return acc
        acc = lax.fori_loop(0, n_pairs_all, body, jnp.full((COUNT_ROWS, tq), -jnp.inf, F32))
        below = jnp.max(acc, axis=0, keepdims=True)
        move = jnp.logical_and(done == 0, c_s < topk)
        return jnp.where(move, below, hi_s), jnp.where(move, c_s + 1, c_s)

    hi_s, c_s = lax.fori_loop(0, MAX_EXTRACT, step_down, (hi, c_hi))
    stepped = jnp.logical_and(done == 0, c_s == topk)
    c_chk = count_ge(jnp.where(stepped, hi_s, lo))
    lo = jnp.where(stepped, hi_s, lo)
    c_lo = jnp.where(stepped, c_chk, c_lo)
    done = jnp.where(c_lo == topk, 1, done)
    _, thr, _, n_ge, _, _ = lax.while_loop(keep_bisecting(all_done), search_step,
                                           (it, lo, hi, c_lo, c_hi, done))

    @pl.when(jnp.max(n_ge) > topk)
    def _():
        need = topk - count_where(lambda x, j: x > thr)

        idx_bits = (sc_ref.shape[0] * tk - 1).bit_length()

        def idx_step(t, c):
            cand = c | jnp.left_shift(jnp.int32(1), idx_bits - 1 - t)
            below = count_where(lambda x, j: jnp.logical_and(x == thr, j * tk + key_row < cand))
            return jnp.where(below < need, cand, c)

        cut = lax.fori_loop(0, idx_bits, idx_step, jnp.zeros((1, tq), I32))

        def demote_block(j, carry):
            x = sc_ref[j]
            drop = jnp.logical_and(x == thr, j * tk + key_row > cut)
            sc_ref[j] = jnp.where(drop, NEG_INF, x)
            return carry

        lax.fori_loop(0, nkb, demote_block, 0)

    slope_pieces = [_bf16_pieces(2.0 ** (-8.0 * (h + 1) / N_HEADS) * LOG2E) for h in range(N_HEADS)]
    slope_l2 = [sum(pieces) for pieces in slope_pieces]
    row_a = lax.broadcasted_iota(I32, (HEAD_DIM, tq), 0)
    for h in range(N_HEADS):
        hs = slice(h * HEAD_DIM, (h + 1) * HEAD_DIM)
        aug = jnp.zeros((HEAD_DIM, tq), F32)
        for r, c in enumerate(slope_pieces[h]):
            aug = jnp.where(row_a == 2 * r, c * POS_RADIX, jnp.where(row_a == 2 * r + 1, c, aug))
        qaug_ref[h, 0:HEAD_DIM, :] = qT_ref[hs, :]
        qaug_ref[h, HEAD_DIM:, :] = aug.astype(BF16)
    lane_p = lax.broadcasted_iota(I32, (tk, HEAD_DIM), 1)
    row_p = lax.broadcasted_iota(I32, (tk, HEAD_DIM), 0)
    pos = jnp.where(lane_p % 2 == 0, row_p // POS_RADIX, row_p % POS_RADIX)
    pos_ref[...] = jnp.where(lane_p < 2 * N_SLOPE_PIECES, pos, 0).astype(F32).astype(BF16)

    l_ref[...] = jnp.zeros(l_ref.shape, F32)
    acc_ref[...] = jnp.zeros(acc_ref.shape, F32)

    key_minus_query = key_row - (q_pos - i * tq)

    def logits_block(j, scores, m_run, slot):
        k0 = pl.multiple_of(j * tk, tk)
        k0f = jnp.asarray(j * tk).astype(F32)
        selected = jnp.logical_and(scores >= thr, key_minus_query <= i * tq - j * tk)
        bias = jnp.where(selected, 0.0, NEG_INF)
        m_blk = []
        for h in range(N_HEADS):
            hs = slice(h * HEAD_DIM, (h + 1) * HEAD_DIM)
            lhs = jnp.concatenate([k_ref[pl.ds(k0, tk), hs], pos_ref[...]], axis=1)
            s = _dot(lhs, qaug_ref[h]) + bias
            s_ref[slot, h] = s
            m_blk.append(jnp.max(s, axis=0, keepdims=True) + k0f * slope_l2[h])
        m_new = jnp.maximum(m_run, jnp.concatenate(m_blk, axis=0))
        return m_new, jnp.exp2(m_run - m_new)

    def values_block(j, m_j, alpha, slot):
        sums = []
        for h in range(N_HEADS):
            hs = slice(h * HEAD_DIM, (h + 1) * HEAD_DIM)
            m_local = m_j[h:h + 1, :] - jnp.asarray(j * tk).astype(F32) * slope_l2[h]
            p = jnp.exp2(s_ref[slot, h] - m_local)
            sums.append(jnp.sum(p, axis=0, keepdims=True))
            pv = _dot(vT_ref[j, hs, :], p.astype(BF16))
            acc_ref[hs, :] = alpha[h:h + 1, :] * acc_ref[hs, :] + pv
        l_ref[...] = alpha * l_ref[...] + jnp.concatenate(sums, axis=0)

    def attn_step(j, carry, slot, scores_next):
        m_j, alpha_j = carry
        nxt = logits_block(j + 1, scores_next, m_j, 1 - slot)
        values_block(j, m_j, alpha_j, slot)
        return nxt

    def attn_pair(p, carry):
        scores_a = sc_ref[2 * p + 1]
        scores_b = sc_ref[2 * p + 2]
        carry = attn_step(2 * p, carry, 0, scores_a)
        score_next(2 * p)
        carry = attn_step(2 * p + 1, carry, 1, scores_b)
        score_next(2 * p + 1)
        return carry

    gmax_ref[...] = jnp.full(gmax_ref.shape, -jnp.inf, F32)
    first = logits_block(0, sc_ref[0], jnp.full((N_HEADS, tq), -jnp.inf, F32), 0)
    n_pairs = (nkb - 1) // 2
    m_c, alpha_c = lax.fori_loop(0, n_pairs, attn_pair, first)
    j_rest = 2 * n_pairs

    @pl.when(j_rest == nkb - 1)
    def _():
        values_block(j_rest, m_c, alpha_c, 0)
        score_next(j_rest)

    @pl.when(j_rest < nkb - 1)
    def _():
        m_l, alpha_l = attn_step(j_rest, (m_c, alpha_c), 0, sc_ref[j_rest + 1])
        values_block(j_rest + 1, m_l, alpha_l, 1)
        score_next(j_rest)
        score_next(j_rest + 1)

    @pl.when(i + 1 < pl.num_programs(1))
    def _():
        score_block(nkb, qiTn_ref, wiTn_ref, (i + 1) * tq)

    inv_l = 1.0 / l_ref[...]
    for h in range(N_HEADS):
        hs = slice(h * HEAD_DIM, (h + 1) * HEAD_DIM)
        o_ref[:, hs] = (acc_ref[hs, :] * inv_l[h:h + 1, :]).T.astype(BF16)


def _dsa(qT, qiT, wiT, ki, k, vT, batch, seq, tq, tk):
    d_attn = qT.shape[1]
    nq = seq // tq
    topk = min(TOPK_MAX, seq // 4)
    assert tk >= topk and tk % COUNT_ROWS == 0 and seq % tk == 0 and tq == tk
    col = lambda b, i: (b, 0, i)
    col_next = lambda b, i: (b, 0, jnp.minimum(i + 1, nq - 1))
    kernel = functools.partial(_dsa_kernel, tq=tq, tk=tk, topk=topk)
    return pl.pallas_call(
        kernel,
        grid=(batch, nq),
        in_specs=[
            pl.BlockSpec((None, d_attn, tq), col),
            pl.BlockSpec((None, qiT.shape[1], tq), col),
            pl.BlockSpec((None, N_IDX_HEADS, tq), col),
            pl.BlockSpec((None, qiT.shape[1], tq), col_next),
            pl.BlockSpec((None, N_IDX_HEADS, tq), col_next),
            _resident((seq, IDX_DIM), lambda b, i: (b, 0)),
            _resident((seq, d_attn), lambda b, i: (b, 0)),
            _resident((None, seq // tk, d_attn, tk), lambda b, i: (b, 0, 0, 0)),
        ],
        out_specs=pl.BlockSpec((tq, d_attn), lambda b, i: (b * nq + i, 0)),
        out_shape=jax.ShapeDtypeStruct((batch * seq, d_attn), BF16),
        scratch_shapes=[
            pltpu.VMEM((seq // tk, tk, tq), F32),
            pltpu.VMEM((tk, tq), F32),
            pltpu.VMEM((N_HEADS, tq), F32),
            pltpu.VMEM((d_attn, tq), F32),
            pltpu.VMEM((N_HEADS, 2 * HEAD_DIM, tq), BF16),
            pltpu.VMEM((2, N_HEADS, tk, tq), F32),
            pltpu.VMEM((tk, HEAD_DIM), BF16),
        ],
        compiler_params=pltpu.CompilerParams(
            dimension_semantics=("arbitrary", "arbitrary"), vmem_limit_bytes=VMEM_LIMIT),
        name="dsa",
    )(qT, qiT, wiT, qiT, wiT, ki, k, vT)


def _merge_ln_kernel(h_ref, oa_ref, or_ref, ga_ref, gb_ref, wpa, wpr, wout, g_ref, b_ref, o_ref):
    ya = _dot(oa_ref[...], wpa[...])
    yr = _dot(or_ref[...], wpr[...])
    y = jax.nn.sigmoid(ga_ref[...]) * ya + jax.nn.sigmoid(gb_ref[...]) * yr
    mix = _dot(y.astype(BF16), wout[...])
    o_ref[...] = _layer_norm(ALPHA * h_ref[...] + mix, g_ref[...], b_ref[...])


def _merge_ln(h, o_attn, o_rnn, gab, wpa, wpr, wout, g, b, tm=512):
    n, d = h.shape
    const = lambda i: (0, 0)
    return pl.pallas_call(
        _merge_ln_kernel,
        grid=(n // tm,),
        in_specs=[
            pl.BlockSpec((tm, d), lambda i: (i, 0)),
            pl.BlockSpec((tm, o_attn.shape[1]), lambda i: (i, 0)),
            pl.BlockSpec((tm, o_rnn.shape[1]), lambda i: (i, 0)),
            pl.BlockSpec((tm, d), lambda i: (i, 0)),
            pl.BlockSpec((tm, d), lambda i: (i, 1)),
            _resident(wpa.shape, const),
            _resident(wpr.shape, const),
            _resident(wout.shape, const),
            _resident((1, d), const),
            _resident((1, d), const),
        ],
        out_specs=pl.BlockSpec((tm, d), lambda i: (i, 0)),
        out_shape=jax.ShapeDtypeStruct((n, d), F32),
        compiler_params=pltpu.CompilerParams(
            dimension_semantics=("arbitrary",), vmem_limit_bytes=VMEM_LIMIT),
        name="merge_ln",
    )(h, o_attn, o_rnn, gab, gab, wpa, wpr, wout, g, b)


def _layer(x2, batch, seq, p, tq=256, tk=256):
    d_attn = N_HEADS * HEAD_DIM
    row = lambda a: a.reshape(1, -1).astype(F32)
    colv = lambda a: a.reshape(-1, 1).astype(F32)

    h1 = _ffn_ln(x2, p['ffn1_w_gate'].astype(BF16), p['ffn1_w_up'].astype(BF16),
                 p['ffn1_w_down'].astype(BF16), row(p['ln1_g']), row(p['ln1_b']))

    w_in, b_in = p['w_in'], p['b_in']
    o = 0
    wq, bq = w_in[:, o:o + d_attn], b_in[o:o + d_attn]; o += d_attn
    wk, bk = w_in[:, o:o + d_attn], b_in[o:o + d_attn]; o += d_attn
    wv, bv = w_in[:, o:o + d_attn], b_in[o:o + d_attn]; o += d_attn
    n_qi = N_IDX_HEADS * IDX_DIM
    wqi, bqi = w_in[:, o:o + n_qi], b_in[o:o + n_qi]; o += n_qi
    wki, bki = w_in[:, o:o + IDX_DIM], b_in[o:o + IDX_DIM]; o += IDX_DIM
    wwi, bwi = w_in[:, o:o + N_IDX_HEADS], b_in[o:o + N_IDX_HEADS]; o += N_IDX_HEADS
    d_rnn = N_RNN_BLOCKS * RNN_BLOCK
    wxg, bxg = w_in[:, o:o + 2 * d_rnn], b_in[o:o + 2 * d_rnn]; o += 2 * d_rnn
    wab, bab = w_in[:, o:], b_in[o:]
    ws = [w.astype(BF16) for w in (wq.T, wk, wv.T, wqi.T, wki, wwi.T, wxg, wab)]
    bs = [colv(bq), row(bk), colv(bv), colv(bqi), row(bki), colv(bwi), row(bxg), row(bab)]
    w_gates = jnp.concatenate([p['lru_w_a'], p['lru_w_x']], axis=-1).astype(BF16)
    rnn = [p['conv_w'].astype(F32), row(p['conv_b']), w_gates, row(p['lru_b_a']), row(p['lru_b_x']),
           row(p['lru_lambda'])]
    qT, k, vT, qiT, ki, wiT, gab, o_rnn = _in_proj(h1, ws, bs, rnn, batch, seq, tk)

    o_attn = _dsa(qT, qiT, wiT, ki, k, vT, batch, seq, tq, tk)

    h2 = _merge_ln(h1, o_attn, o_rnn, gab, p['w_proj_attn'].astype(BF16),
                   p['w_proj_rnn'].astype(BF16), p['w_out'].astype(BF16),
                   row(p['ln2_g']), row(p['ln2_b']))

    return _ffn_ln(h2, p['ffn2_w_gate'].astype(BF16), p['ffn2_w_up'].astype(BF16),
                   p['ffn2_w_down'].astype(BF16), row(p['ln3_g']), row(p['ln3_b']))


_PARAM_NAMES = ('ln1_g', 'ln1_b', 'ffn1_w_gate', 'ffn1_w_up', 'ffn1_w_down', 'w_in', 'b_in', 'conv_w',
                'conv_b', 'lru_w_a', 'lru_b_a', 'lru_w_x', 'lru_b_x', 'lru_lambda', 'w_proj_attn',
                'w_proj_rnn', 'w_out', 'ln2_g', 'ln2_b', 'ffn2_w_gate', 'ffn2_w_up', 'ffn2_w_down',
                'ln3_g', 'ln3_b')


@jax.jit
def _forward(x, *params):
    batch, seq, d = x.shape
    h = x.reshape(batch * seq, d)
    for l in range(DEPTH):
        p = {name: a[l] for name, a in zip(_PARAM_NAMES, params)}
        h = _layer(h, batch, seq, p)
    return h.reshape(batch, seq, d)


def kernel(x, ln1_g, ln1_b, ffn1_w_gate, ffn1_w_up, ffn1_w_down, w_in, b_in, conv_w, conv_b, lru_w_a, lru_b_a, lru_w_x, lru_b_x, lru_lambda, w_proj_attn, w_proj_rnn, w_out, ln2_g, ln2_b, ffn2_w_gate, ffn2_w_up, ffn2_w_down, ln3_g, ln3_b):
    return _forward(x, ln1_g, ln1_b, ffn1_w_gate, ffn1_w_up, ffn1_w_down, w_in, b_in, conv_w, conv_b,
                    lru_w_a, lru_b_a, lru_w_x, lru_b_x, lru_lambda, w_proj_attn, w_proj_rnn, w_out,
                    ln2_g, ln2_b, ffn2_w_gate, ffn2_w_up, ffn2_w_down, ln3_g, ln3_b)
```

```python
import functools

import jax
import jax.numpy as jnp
import numpy as np
from jax import lax
from jax.experimental import pallas as pl
from jax.experimental.pallas import tpu as pltpu

F32 = jnp.float32
BF16 = jnp.bfloat16
I32 = jnp.int32

N_HEADS = 8
HEAD_DIM = 128
N_IDX_HEADS = 16
IDX_DIM = 64
TOPK_MAX = 256
N_RNN_BLOCKS = 8
RNN_BLOCK = 128
CONV_WIDTH = 4
LRU_C = 8.0
LN_EPS = 1e-5
DEPTH = 1
ALPHA = (2.0 * DEPTH) ** 0.25
NEG_INF = -1e30

V7X_VMEM_BYTES = 64 * 1024 * 1024
V7X_SUBLANES = 8
V7X_LANES = 128
VMEM_LIMIT = V7X_VMEM_BYTES // 8 * 7
FFN_ROWS = 512
PROJ_ROWS = 256
DSA_BLOCK = 256


def _resident(shape, index_map):
    return pl.BlockSpec(shape, index_map, pipeline_mode=pl.Buffered(1))


def _layer_norm(y, g, b):
    mu = jnp.mean(y, axis=-1, keepdims=True)
    d = y - mu
    var = jnp.mean(d * d, axis=-1, keepdims=True)
    return d * lax.rsqrt(var + LN_EPS) * g + b


def _dot(a, b):
    return jnp.dot(a, b, preferred_element_type=F32)


def _dot_nt(a, b):
    return lax.dot_general(a, b, (((1,), (1,)), ((), ())), preferred_element_type=F32)


FF_CHUNK = 256


def _ffn_ln_kernel(x_ref, wg_ref, wu_ref, wd_ref, g_ref, b_ref, o_ref, acc_ref):
    x = x_ref[...]
    xb = x.astype(BF16)
    d_ff = wg_ref.shape[1]
    for c in range(d_ff // FF_CHUNK):
        sl = slice(c * FF_CHUNK, (c + 1) * FF_CHUNK)
        g = _dot(xb, wg_ref[:, sl])
        u = _dot(xb, wu_ref[:, sl])
        hid = (g * jax.nn.sigmoid(g) * u).astype(BF16)
        part = _dot(hid, wd_ref[sl, :])
        if c == 0:
            acc_ref[...] = part
        else:
            acc_ref[...] += part
    y = ALPHA * x + 0.5 * acc_ref[...]
    o_ref[...] = _layer_norm(y, g_ref[...], b_ref[...])


def _ffn_ln(x, wg, wu, wd, g, b, tm=FFN_ROWS):
    n, d = x.shape
    d_ff = wg.shape[1]
    assert n % tm == 0 and d_ff % FF_CHUNK == 0
    return pl.pallas_call(
        _ffn_ln_kernel,
        grid=(n // tm,),
        in_specs=[
            pl.BlockSpec((tm, d), lambda i: (i, 0)),
            _resident((d, d_ff), lambda i: (0, 0)),
            _resident((d, d_ff), lambda i: (0, 0)),
            _resident((d_ff, d), lambda i: (0, 0)),
            _resident((1, d), lambda i: (0, 0)),
            _resident((1, d), lambda i: (0, 0)),
        ],
        out_specs=pl.BlockSpec((tm, d), lambda i: (i, 0)),
        out_shape=jax.ShapeDtypeStruct((n, d), F32),
        scratch_shapes=[pltpu.VMEM((tm, d), F32)],
        compiler_params=pltpu.CompilerParams(
            dimension_semantics=("arbitrary",), vmem_limit_bytes=VMEM_LIMIT),
        name="ffn_ln",
    )(x, wg, wu, wd, g, b)


CONV_HALO = V7X_SUBLANES
PROJ_PIECE = 256
PIECES_PER_STAGE = 2


def _rglru_coeffs(x, cw_ref, cb_ref, wg_ref, ba_ref, bx_ref, lam_ref, xpad_ref, a_ref, b_ref, between):
    tr = x.shape[0]
    xpad_ref[CONV_HALO:, :] = x
    xc = cb_ref[...] + cw_ref[CONV_WIDTH - 1:CONV_WIDTH, :] * x
    for j in range(CONV_WIDTH - 1):
        back = CONV_WIDTH - 1 - j
        xc = xc + cw_ref[j:j + 1, :] * xpad_ref[CONV_HALO - back:CONV_HALO - back + tr, :]
    xpad_ref[0:CONV_HALO, :] = x[tr - CONV_HALO:, :]

    xcb = xc.astype(BF16)
    sp = jax.nn.softplus(-lam_ref[...])
    for n in range(N_RNN_BLOCKS):
        cs = slice(n * RNN_BLOCK, (n + 1) * RNN_BLOCK)
        g2 = _dot(xcb[:, cs], wg_ref[n])
        r = jax.nn.sigmoid(g2[:, :RNN_BLOCK] + ba_ref[:, cs])
        ig = jax.nn.sigmoid(g2[:, RNN_BLOCK:] + bx_ref[:, cs])
        log_a = -LRU_C * r * sp[:, cs]
        a = jnp.exp(log_a)
        one_minus_a2 = jnp.tanh(-log_a) * (a * a + 1.0)
        a_ref[:, cs] = a
        b_ref[:, cs] = jnp.sqrt(one_minus_a2) * (ig * xc[:, cs])
        between[n]()


def _rglru_scan(a_ref, b_ref, h_ref):
    h = h_ref[...]
    rows = V7X_SUBLANES
    row = lax.broadcasted_iota(I32, (rows, a_ref.shape[1]), 0)
    for g in range(a_ref.shape[0] // rows):
        a = a_ref[g * rows:(g + 1) * rows, :]
        b = b_ref[g * rows:(g + 1) * rows, :]
        d = 1
        while d < rows:
            prev_a = pltpu.roll(a, d, axis=0)
            prev_b = pltpu.roll(b, d, axis=0)
            b = jnp.where(row >= d, a * prev_b + b, b)
            a = jnp.where(row >= d, a * prev_a, a)
            d *= 2
        hs = a * h + b
        a_ref[g * rows:(g + 1) * rows, :] = hs
        h = hs[rows - 1:rows, :]
    h_ref[...] = h


def _in_proj_kernel(h_ref, wqT, wk, wvT, wqiT, wki, wwiT, wxg, wab,
                    bqT, bk, bvT, bqiT, bki, bwiT, bxg, bab,
                    cw_ref, cb_ref, wg_ref, ba_ref, bx_ref, lam_ref,
                    qT_o, k_o, vT_o, qiT_o, ki_o, wiT_o, gab_o, ornn_o,
                    xpad_ref, a_ref, b_ref, hst_ref, *, tk):
    i = pl.program_id(1)

    @pl.when(i == 0)
    def _():
        xpad_ref[0:CONV_HALO, :] = jnp.zeros((CONV_HALO, xpad_ref.shape[1]), F32)
        hst_ref[...] = jnp.zeros_like(hst_ref)

    hb = h_ref[...].astype(BF16)
    tm = hb.shape[0]
    d_rnn = cw_ref.shape[1]
    xg = _dot(hb, wxg[...]) + bxg[...]

    def rows_piece(w, b_, out, lo):
        def run():
            sl = slice(lo, lo + PROJ_PIECE)
            out[:, sl] = (_dot(hb, w[:, sl]) + b_[:, sl]).astype(out.dtype)
        return run

    def cols_piece(wT, bT, store, lo, scale=1.0):
        def run():
            sl = slice(lo, lo + PROJ_PIECE)
            store(sl, (_dot_nt(wT[sl, :], hb) + bT[sl, :]) * scale)
        return run

    def store_qT(sl, v):
        qT_o[sl, :] = v.astype(BF16)

    def store_qiT(sl, v):
        qiT_o[sl, :] = v.astype(BF16)

    def store_vT(sl, v):
        for c in range(tm // tk):
            vT_o[c, sl, :] = v[:, c * tk:(c + 1) * tk].astype(BF16)

    def small_pieces():
        ki_o[...] = (_dot(hb, wki[...]) + bki[...]).astype(BF16)
        wiT_o[...] = (_dot_nt(wwiT[...], hb) + bwiT[...]) * (N_IDX_HEADS ** -0.5 * IDX_DIM ** -0.5)

    q_scale = HEAD_DIM ** -0.5 * LOG2E
    pieces = [rows_piece(wk, bk, k_o, lo) for lo in range(0, k_o.shape[1], PROJ_PIECE)]
    pieces += [rows_piece(wab, bab, gab_o, lo) for lo in range(0, gab_o.shape[1], PROJ_PIECE)]
    pieces += [cols_piece(wqT, bqT, store_qT, lo, q_scale) for lo in range(0, qT_o.shape[0], PROJ_PIECE)]
    pieces += [cols_piece(wqiT, bqiT, store_qiT, lo) for lo in range(0, qiT_o.shape[0], PROJ_PIECE)]
    pieces += [cols_piece(wvT, bvT, store_vT, lo) for lo in range(0, wvT.shape[0], PROJ_PIECE)]
    pieces += [small_pieces]

    def run_pieces(lo, hi):
        def run():
            for piece in pieces[lo:hi]:
                piece()
        return run

    _rglru_coeffs(xg[:, :d_rnn], cw_ref, cb_ref, wg_ref, ba_ref, bx_ref, lam_ref, xpad_ref, a_ref, b_ref,
                  [run_pieces(n * PIECES_PER_STAGE, (n + 1) * PIECES_PER_STAGE) for n in range(N_RNN_BLOCKS)])
    _rglru_scan(a_ref, b_ref, hst_ref)
    ornn_o[...] = (a_ref[...] * jax.nn.gelu(xg[:, d_rnn:], approximate=True)).astype(BF16)
    run_pieces(N_RNN_BLOCKS * PIECES_PER_STAGE, len(pieces))()


def _in_proj(h, ws, bs, rnn, batch, seq, tk, tm=PROJ_ROWS):
    n, d = h.shape
    nt = seq // tm
    d_attn = N_HEADS * HEAD_DIM
    d_qi = N_IDX_HEADS * IDX_DIM
    d_rnn = rnn[0].shape[1]
    row = lambda b, i: (b * nt + i, 0)
    col = lambda b, i: (b, 0, i)
    const = lambda b, i: (0, 0)
    in_specs = [pl.BlockSpec((tm, d), row)]
    in_specs += [_resident(w.shape, const) for w in ws]
    in_specs += [_resident(b_.shape, const) for b_ in bs]
    in_specs += [_resident(p.shape, (lambda b, i, nd=p.ndim: (0,) * nd)) for p in rnn]
    out_shape = [
        jax.ShapeDtypeStruct((batch, d_attn, seq), BF16),
        jax.ShapeDtypeStruct((n, d_attn), BF16),
        jax.ShapeDtypeStruct((batch, seq // tk, d_attn, tk), BF16),
        jax.ShapeDtypeStruct((batch, d_qi, seq), BF16),
        jax.ShapeDtypeStruct((n, IDX_DIM), BF16),
        jax.ShapeDtypeStruct((batch, N_IDX_HEADS, seq), F32),
        jax.ShapeDtypeStruct((n, ws[7].shape[1]), F32),
        jax.ShapeDtypeStruct((n, d_rnn), BF16),
    ]
    out_specs = [
        pl.BlockSpec((None, d_attn, tm), col),
        pl.BlockSpec((tm, d_attn), row),
        pl.BlockSpec((None, tm // tk, d_attn, tk), lambda b, i: (b, i, 0, 0)),
        pl.BlockSpec((None, d_qi, tm), col),
        pl.BlockSpec((tm, IDX_DIM), row),
        pl.BlockSpec((None, N_IDX_HEADS, tm), col),
        pl.BlockSpec((tm, ws[7].shape[1]), row),
        pl.BlockSpec((tm, d_rnn), row),
    ]
    return pl.pallas_call(
        functools.partial(_in_proj_kernel, tk=tk),
        grid=(batch, nt),
        in_specs=in_specs,
        out_specs=out_specs,
        out_shape=out_shape,
        scratch_shapes=[pltpu.VMEM((tm + CONV_HALO, d_rnn), F32), pltpu.VMEM((tm, d_rnn), F32),
                        pltpu.VMEM((tm, d_rnn), F32), pltpu.VMEM((1, d_rnn), F32)],
        compiler_params=pltpu.CompilerParams(
            dimension_semantics=("arbitrary", "arbitrary"), vmem_limit_bytes=VMEM_LIMIT),
        name="in_proj",
    )(h, *ws, *bs, *rnn)


COUNT_ROWS = 64
COUNT_UNKNOWN = 2 ** 30
MAX_EXTRACT = 3
MIN_SEARCH_ITERS = 10
MAX_SEARCH_ITERS = 400
LOG2E = 1.4426950408889634
POS_RADIX = 64
N_SLOPE_PIECES = 3


def _bf16_pieces(c):
    out = []
    for _ in range(N_SLOPE_PIECES):
        piece = float(np.asarray(c, np.float32).astype(BF16).astype(np.float32))
        out.append(piece)
        c = c - piece
    return out


def _dsa_kernel(qT_ref, qiT_ref, wiT_ref, qiTn_ref, wiTn_ref, ki_ref, k_ref, vT_ref, o_ref,
                sc_ref, gmax_ref, l_ref, acc_ref, qaug_ref, s_ref, pos_ref, *, tq, tk, topk):
    i = pl.program_id(1)
    nkb = i + 1
    q_pos = i * tq + lax.broadcasted_iota(I32, (tk, tq), 1)
    key_row = lax.broadcasted_iota(I32, (tk, tq), 0)
    q_row = i * tq + lax.broadcasted_iota(I32, (1, tq), 1)

    def score_block(j, qi_ref, wi_ref, diag_first_query):
        k0 = pl.multiple_of(j * tk, tk)
        ki = ki_ref[pl.ds(k0, tk), :]
        acc = jnp.zeros((tk, tq), F32)
        for h in range(N_IDX_HEADS):
            r = _dot(ki, qi_ref[h * IDX_DIM:(h + 1) * IDX_DIM, :])
            acc = acc + wi_ref[h:h + 1, :] * jnp.maximum(r, 0.0)
        if diag_first_query is not None:
            acc = jnp.where(k0 + key_row <= diag_first_query + (q_pos - i * tq), acc, NEG_INF)
        sc_ref[j] = acc
        gmax_ref[...] = jnp.maximum(gmax_ref[...], acc)

    def score_next(j):
        score_block(j, qiTn_ref, wiTn_ref, None)

    @pl.when(i == 0)
    def _():
        gmax_ref[...] = jnp.full(gmax_ref.shape, -jnp.inf, F32)
        score_block(0, qiT_ref, wiT_ref, 0)

    n_pairs_all = (nkb + 1) // 2

    @pl.when(nkb % 2 == 1)
    def _():
        sc_ref[nkb] = jnp.full((tk, tq), NEG_INF, F32)

    def count_where(pred):
        def body(p, acc):
            for j in (2 * p, 2 * p + 1):
                m = jnp.where(pred(sc_ref[j], j), 1, 0).astype(I32)
                for r in range(tk // COUNT_ROWS):
                    acc = acc + m[r * COUNT_ROWS:(r + 1) * COUNT_ROWS, :]
            return acc
        acc = lax.fori_loop(0, n_pairs_all, body, jnp.zeros((COUNT_ROWS, tq), I32))
        return jnp.sum(acc, axis=0, keepdims=True)

    def count_ge(v):
        return count_where(lambda x, j: x >= v)

    gmax = gmax_ref[...]
    lo = jnp.min(gmax, axis=0, keepdims=True)
    ub = jnp.max(gmax, axis=0, keepdims=True)
    hi = ub + jnp.maximum(jnp.abs(ub) * 2.0 ** -20, 1e-30)
    all_rows = q_row + 1 <= topk
    lo = jnp.where(all_rows, NEG_INF, lo)
    c_lo = jnp.where(all_rows, topk, COUNT_UNKNOWN)
    c_hi = jnp.zeros((1, tq), I32)
    done = (c_lo == topk).astype(I32)

    def search_step(st):
        it, lo, hi, c_lo, c_hi, done = st
        v = 0.5 * lo + 0.5 * hi
        adjacent = jnp.logical_or(v <= lo, v >= hi)
        c = count_ge(v)
        live = jnp.logical_and(done == 0, jnp.logical_not(adjacent))
        up = jnp.logical_and(live, c >= topk)
        dn = jnp.logical_and(live, c < topk)
        lo = jnp.where(up, v, lo)
        c_lo = jnp.where(up, c, c_lo)
        hi = jnp.where(dn, v, hi)
        c_hi = jnp.where(dn, c, c_hi)
        done = jnp.where(jnp.logical_or(adjacent, c_lo == topk), 1, done)
        return it + 1, lo, hi, c_lo, c_hi, done

    def all_done(st):
        return jnp.min(st[5]) == 1

    def near_top(st):
        it, lo, hi, c_lo, c_hi, done = st
        return jnp.min(jnp.where(jnp.logical_or(done == 1, topk - c_hi <= MAX_EXTRACT), 1, 0)) == 1

    def keep_bisecting(stop):
        return lambda st: jnp.logical_and(st[0] < MAX_SEARCH_ITERS, jnp.logical_not(stop(st)))

    st = (jnp.int32(0), lo, hi, c_lo, c_hi, done)
    st = lax.fori_loop(0, MIN_SEARCH_ITERS, lambda t, st: search_step(st), st)
    st = lax.while_loop(keep_bisecting(near_top), search_step, st)
    it, lo, hi, c_lo, c_hi, done = st

    def step_down(t, sd):
        hi_s, c_s = sd
        def body(p, acc):
            for j in (2 * p, 2 * p + 1):
                x = jnp.where(sc_ref[j] < hi_s, sc_ref[j], -jnp.inf)
                for r in range(tk // COUNT_ROWS):
                    acc = jnp.maximum(acc, x[r * COUNT_ROWS:(r + 1) * COUNT_ROWS, :])
            return acc
        acc = lax.fori_loop(0, n_pairs_all, body, jnp.full((COUNT_ROWS, tq), -jnp.inf, F32))
        below = jnp.max(acc, axis=0, keepdims=True)
        move = jnp.logical_and(done == 0, c_s < topk)
        return jnp.where(move, below, hi_s), jnp.where(move, c_s + 1, c_s)

    hi_s, c_s = lax.fori_loop(0, MAX_EXTRACT, step_down, (hi, c_hi))
    stepped = jnp.logical_and(done == 0, c_s == topk)
    c_chk = count_ge(jnp.where(stepped, hi_s, lo))
    lo = jnp.where(stepped, hi_s, lo)
    c_lo = jnp.where(stepped, c_chk, c_lo)
    done = jnp.where(c_lo == topk, 1, done)
    _, thr, _, n_ge, _, _ = lax.while_loop(keep_bisecting(all_done), search_step,
                                           (it, lo, hi, c_lo, c_hi, done))

    @pl.when(jnp.max(n_ge) > topk)
    def _():
        need = topk - count_where(lambda x, j: x > thr)

        idx_bits = (sc_ref.shape[0] * tk - 1).bit_length()

        def idx_step(t, c):
            cand = c | jnp.left_shift(jnp.int32(1), idx_bits - 1 - t)
            below = count_where(lambda x, j: jnp.logical_and(x == thr, j * tk + key_row < cand))
            return jnp.where(below < need, cand, c)

        cut = lax.fori_loop(0, idx_bits, idx_step, jnp.zeros((1, tq), I32))

        def demote_block(j, carry):
            x = sc_ref[j]
            drop = jnp.logical_and(x == thr, j * tk + key_row > cut)
            sc_ref[j] = jnp.where(drop, NEG_INF, x)
            return carry

        lax.fori_loop(0, nkb, demote_block, 0)

    slope_pieces = [_bf16_pieces(2.0 ** (-8.0 * (h + 1) / N_HEADS) * LOG2E) for h in range(N_HEADS)]
    slope_l2 = [sum(pieces) for pieces in slope_pieces]
    row_a = lax.broadcasted_iota(I32, (HEAD_DIM, tq), 0)
    for h in range(N_HEADS):
        hs = slice(h * HEAD_DIM, (h + 1) * HEAD_DIM)
        aug = jnp.zeros((HEAD_DIM, tq), F32)
        for r, c in enumerate(slope_pieces[h]):
            aug = jnp.where(row_a == 2 * r, c * POS_RADIX, jnp.where(row_a == 2 * r + 1, c, aug))
        qaug_ref[h, 0:HEAD_DIM, :] = qT_ref[hs, :]
        qaug_ref[h, HEAD_DIM:, :] = aug.astype(BF16)
    lane_p = lax.broadcasted_iota(I32, (tk, HEAD_DIM), 1)
    row_p = lax.broadcasted_iota(I32, (tk, HEAD_DIM), 0)
    pos = jnp.where(lane_p % 2 == 0, row_p // POS_RADIX, row_p % POS_RADIX)
    pos_ref[...] = jnp.where(lane_p < 2 * N_SLOPE_PIECES, pos, 0).astype(F32).astype(BF16)

    l_ref[...] = jnp.zeros(l_ref.shape, F32)
    acc_ref[...] = jnp.zeros(acc_ref.shape, F32)

    key_minus_query = key_row - (q_pos - i * tq)

    def logits_block(j, scores, m_run, slot):
        k0 = pl.multiple_of(j * tk, tk)
        k0f = jnp.asarray(j * tk).astype(F32)
        selected = jnp.logical_and(scores >= thr, key_minus_query <= i * tq - j * tk)
        bias = jnp.where(selected, 0.0, NEG_INF)
        m_blk = []
        for h in range(N_HEADS):
            hs = slice(h * HEAD_DIM, (h + 1) * HEAD_DIM)
            lhs = jnp.concatenate([k_ref[pl.ds(k0, tk), hs], pos_ref[...]], axis=1)
            s = _dot(lhs, qaug_ref[h]) + bias
            s_ref[slot, h] = s
            m_blk.append(jnp.max(s, axis=0, keepdims=True) + k0f * slope_l2[h])
        m_new = jnp.maximum(m_run, jnp.concatenate(m_blk, axis=0))
        return m_new, jnp.exp2(m_run - m_new)

    def values_block(j, m_j, alpha, slot):
        sums = []
        for h in range(N_HEADS):
            hs = slice(h * HEAD_DIM, (h + 1) * HEAD_DIM)
            m_local = m_j[h:h + 1, :] - jnp.asarray(j * tk).astype(F32) * slope_l2[h]
            p = jnp.exp2(s_ref[slot, h] - m_local)
            sums.append(jnp.sum(p, axis=0, keepdims=True))
            pv = _dot(vT_ref[j, hs, :], p.astype(BF16))
            acc_ref[hs, :] = alpha[h:h + 1, :] * acc_ref[hs, :] + pv
        l_ref[...] = alpha * l_ref[...] + jnp.concatenate(sums, axis=0)

    def attn_step(j, carry, slot, scores_next):
        m_j, alpha_j = carry
        nxt = logits_block(j + 1, scores_next, m_j, 1 - slot)
        values_block(j, m_j, alpha_j, slot)
        return nxt

    def attn_pair(p, carry):
        scores_a = sc_ref[2 * p + 1]
        scores_b = sc_ref[2 * p + 2]
        carry = attn_step(2 * p, carry, 0, scores_a)
        score_next(2 * p)
        carry = attn_step(2 * p + 1, carry, 1, scores_b)
        score_next(2 * p + 1)
        return carry

    gmax_ref[...] = jnp.full(gmax_ref.shape, -jnp.inf, F32)
    first = logits_block(0, sc_ref[0], jnp.full((N_HEADS, tq), -jnp.inf, F32), 0)
    n_pairs = (nkb - 1) // 2
    m_c, alpha_c = lax.fori_loop(0, n_pairs, attn_pair, first)
    j_rest = 2 * n_pairs

    @pl.when(j_rest == nkb - 1)
    def _():
        values_block(j_rest, m_c, alpha_c, 0)
        score_next(j_rest)

    @pl.when(j_rest < nkb - 1)
    def _():
        m_l, alpha_l = attn_step(j_rest, (m_c, alpha_c), 0, sc_ref[j_rest + 1])
        values_block(j_rest + 1, m_l, alpha_l, 1)
        score_next(j_rest)
        score_next(j_rest + 1)

    @pl.when(i + 1 < pl.num_programs(1))
    def _():
        score_block(nkb, qiTn_ref, wiTn_ref, (i + 1) * tq)

    inv_l = 1.0 / l_ref[...]
    for h in range(N_HEADS):
        hs = slice(h * HEAD_DIM, (h + 1) * HEAD_DIM)
        o_ref[:, hs] = (acc_ref[hs, :] * inv_l[h:h + 1, :]).T.astype(BF16)


def _dsa(qT, qiT, wiT, ki, k, vT, batch, seq, tq, tk):
    d_attn = qT.shape[1]
    nq = seq // tq
    topk = min(TOPK_MAX, seq // 4)
    assert tk >= topk and tk % COUNT_ROWS == 0 and seq % tk == 0 and tq == tk
    col = lambda b, i: (b, 0, i)
    col_next = lambda b, i: (b, 0, jnp.minimum(i + 1, nq - 1))
    kernel = functools.partial(_dsa_kernel, tq=tq, tk=tk, topk=topk)
    return pl.pallas_call(
        kernel,
        grid=(batch, nq),
        in_specs=[
            pl.BlockSpec((None, d_attn, tq), col),
            pl.BlockSpec((None, qiT.shape[1], tq), col),
            pl.BlockSpec((None, N_IDX_HEADS, tq), col),
            pl.BlockSpec((None, qiT.shape[1], tq), col_next),
            pl.BlockSpec((None, N_IDX_HEADS, tq), col_next),
            _resident((seq, IDX_DIM), lambda b, i: (b, 0)),
            _resident((seq, d_attn), lambda b, i: (b, 0)),
            _resident((None, seq // tk, d_attn, tk), lambda b, i: (b, 0, 0, 0)),
        ],
        out_specs=pl.BlockSpec((tq, d_attn), lambda b, i: (b * nq + i, 0)),
        out_shape=jax.ShapeDtypeStruct((batch * seq, d_attn), BF16),
        scratch_shapes=[
            pltpu.VMEM((seq // tk, tk, tq), F32),
            pltpu.VMEM((tk, tq), F32),
            pltpu.VMEM((N_HEADS, tq), F32),
            pltpu.VMEM((d_attn, tq), F32),
            pltpu.VMEM((N_HEADS, 2 * HEAD_DIM, tq), BF16),
            pltpu.VMEM((2, N_HEADS, tk, tq), F32),
            pltpu.VMEM((tk, HEAD_DIM), BF16),
        ],
        compiler_params=pltpu.CompilerParams(
            dimension_semantics=("arbitrary", "arbitrary"), vmem_limit_bytes=VMEM_LIMIT),
        name="dsa",
    )(qT, qiT, wiT, qiT, wiT, ki, k, vT)


def _merge_ln_kernel(h_ref, oa_ref, or_ref, ga_ref, gb_ref, wpa, wpr, wout, g_ref, b_ref, o_ref):
    ya = _dot(oa_ref[...], wpa[...])
    yr = _dot(or_ref[...], wpr[...])
    y = jax.nn.sigmoid(ga_ref[...]) * ya + jax.nn.sigmoid(gb_ref[...]) * yr
    mix = _dot(y.astype(BF16), wout[...])
    o_ref[...] = _layer_norm(ALPHA * h_ref[...] + mix, g_ref[...], b_ref[...])


def _merge_ln(h, o_attn, o_rnn, gab, wpa, wpr, wout, g, b, tm=FFN_ROWS):
    n, d = h.shape
    const = lambda i: (0, 0)
    return pl.pallas_call(
        _merge_ln_kernel,
        grid=(n // tm,),
        in_specs=[
            pl.BlockSpec((tm, d), lambda i: (i, 0)),
            pl.BlockSpec((tm, o_attn.shape[1]), lambda i: (i, 0)),
            pl.BlockSpec((tm, o_rnn.shape[1]), lambda i: (i, 0)),
            pl.BlockSpec((tm, d), lambda i: (i, 0)),
            pl.BlockSpec((tm, d), lambda i: (i, 1)),
            _resident(wpa.shape, const),
            _resident(wpr.shape, const),
            _resident(wout.shape, const),
            _resident((1, d), const),
            _resident((1, d), const),
        ],
        out_specs=pl.BlockSpec((tm, d), lambda i: (i, 0)),
        out_shape=jax.ShapeDtypeStruct((n, d), F32),
        compiler_params=pltpu.CompilerParams(
            dimension_semantics=("arbitrary",), vmem_limit_bytes=VMEM_LIMIT),
        name="merge_ln",
    )(h, o_attn, o_rnn, gab, gab, wpa, wpr, wout, g, b)


def _layer(x2, batch, seq, p, tq=DSA_BLOCK, tk=DSA_BLOCK):
    d_attn = N_HEADS * HEAD_DIM
    row = lambda a: a.reshape(1, -1).astype(F32)
    colv = lambda a: a.reshape(-1, 1).astype(F32)

    h1 = _ffn_ln(x2, p['ffn1_w_gate'].astype(BF16), p['ffn1_w_up'].astype(BF16),
                 p['ffn1_w_down'].astype(BF16), row(p['ln1_g']), row(p['ln1_b']))

    w_in, b_in = p['w_in'].astype(BF16), p['b_in']
    o = 0
    wq, bq = w_in[:, o:o + d_attn], b_in[o:o + d_attn]; o += d_attn
    wk, bk = w_in[:, o:o + d_attn], b_in[o:o + d_attn]; o += d_attn
    wv, bv = w_in[:, o:o + d_attn], b_in[o:o + d_attn]; o += d_attn
    n_qi = N_IDX_HEADS * IDX_DIM
    wqi, bqi = w_in[:, o:o + n_qi], b_in[o:o + n_qi]; o += n_qi
    wki, bki = w_in[:, o:o + IDX_DIM], b_in[o:o + IDX_DIM]; o += IDX_DIM
    wwi, bwi = w_in[:, o:o + N_IDX_HEADS], b_in[o:o + N_IDX_HEADS]; o += N_IDX_HEADS
    d_rnn = N_RNN_BLOCKS * RNN_BLOCK
    wxg, bxg = w_in[:, o:o + 2 * d_rnn], b_in[o:o + 2 * d_rnn]; o += 2 * d_rnn
    wab, bab = w_in[:, o:], b_in[o:]
    ws = [w.astype(BF16) for w in (wq.T, wk, wv.T, wqi.T, wki, wwi.T, wxg, wab)]
    bs = [colv(bq), row(bk), colv(bv), colv(bqi), row(bki), colv(bwi), row(bxg), row(bab)]
    w_gates = jnp.concatenate([p['lru_w_a'], p['lru_w_x']], axis=-1).astype(BF16)
    rnn = [p['conv_w'].astype(F32), row(p['conv_b']), w_gates, row(p['lru_b_a']), row(p['lru_b_x']),
           row(p['lru_lambda'])]
    qT, k, vT, qiT, ki, wiT, gab, o_rnn = _in_proj(h1, ws, bs, rnn, batch, seq, tk)

    o_attn = _dsa(qT, qiT, wiT, ki, k, vT, batch, seq, tq, tk)

    h2 = _merge_ln(h1, o_attn, o_rnn, gab, p['w_proj_attn'].astype(BF16),
                   p['w_proj_rnn'].astype(BF16), p['w_out'].astype(BF16),
                   row(p['ln2_g']), row(p['ln2_b']))

    return _ffn_ln(h2, p['ffn2_w_gate'].astype(BF16), p['ffn2_w_up'].astype(BF16),
                   p['ffn2_w_down'].astype(BF16), row(p['ln3_g']), row(p['ln3_b']))


_PARAM_NAMES = ('ln1_g', 'ln1_b', 'ffn1_w_gate', 'ffn1_w_up', 'ffn1_w_down', 'w_in', 'b_in', 'conv_w',
                'conv_b', 'lru_w_a', 'lru_b_a', 'lru_w_x', 'lru_b_x', 'lru_lambda', 'w_proj_attn',
                'w_proj_rnn', 'w_out', 'ln2_g', 'ln2_b', 'ffn2_w_gate', 'ffn2_w_up', 'ffn2_w_down',
                'ln3_g', 'ln3_b')


@jax.jit
def _forward(x, *params):
    batch, seq, d = x.shape
    h = x.reshape(batch * seq, d)
    for l in range(DEPTH):
        p = {name: a[l] for name, a in zip(_PARAM_NAMES, params)}
        h = _layer(h, batch, seq, p)
    return h.reshape(batch, seq, d)


def kernel(x, ln1_g, ln1_b, ffn1_w_gate, ffn1_w_up, ffn1_w_down, w_in, b_in, conv_w, conv_b, lru_w_a, lru_b_a, lru_w_x, lru_b_x, lru_lambda, w_proj_attn, w_proj_rnn, w_out, ln2_g, ln2_b, ffn2_w_gate, ffn2_w_up, ffn2_w_down, ln3_g, ln3_b):
    return _forward(x, ln1_g, ln1_b, ffn1_w_gate, ffn1_w_up, ffn1_w_down, w_in, b_in, conv_w, conv_b,
                    lru_w_a, lru_b_a, lru_w_x, lru_b_x, lru_lambda, w_proj_attn, w_proj_rnn, w_out,
                    ln2_g, ln2_b, ffn2_w_gate, ffn2_w_up, ffn2_w_down, ln3_g, ln3_b)
```

```python
import functools

import jax
import jax.numpy as jnp
import numpy as np
from jax import lax
from jax.experimental import pallas as pl
from jax.experimental.pallas import tpu as pltpu

F32 = jnp.float32
BF16 = jnp.bfloat16
I32 = jnp.int32

N_HEADS = 8
HEAD_DIM = 128
N_IDX_HEADS = 16
IDX_DIM = 64
TOPK_MAX = 256
N_RNN_BLOCKS = 8
RNN_BLOCK = 128
CONV_WIDTH = 4
LRU_C = 8.0
LN_EPS = 1e-5
DEPTH = 1
ALPHA = (2.0 * DEPTH) ** 0.25
NEG_INF = -1e30

V7X_VMEM_BYTES = 64 * 1024 * 1024
V7X_SUBLANES = 8
V7X_LANES = 128
VMEM_LIMIT = V7X_VMEM_BYTES // 8 * 7
FFN_ROWS = 512
PROJ_ROWS = 256
DSA_BLOCK = 256


def _resident(shape, index_map):
    return pl.BlockSpec(shape, index_map, pipeline_mode=pl.Buffered(1))


def _layer_norm(y, g, b):
    mu = jnp.mean(y, axis=-1, keepdims=True)
    d = y - mu
    var = jnp.mean(d * d, axis=-1, keepdims=True)
    return d * lax.rsqrt(var + LN_EPS) * g + b


def _dot(a, b):
    return jnp.dot(a, b, preferred_element_type=F32)


def _dot_nt(a, b):
    return lax.dot_general(a, b, (((1,), (1,)), ((), ())), preferred_element_type=F32)


FF_CHUNK = 256


def _ffn_ln_kernel(x_ref, wg_ref, wu_ref, wd_ref, g_ref, b_ref, o_ref, acc_ref):
    x = x_ref[...]
    xb = x.astype(BF16)
    d_ff = wg_ref.shape[1]
    for c in range(d_ff // FF_CHUNK):
        sl = slice(c * FF_CHUNK, (c + 1) * FF_CHUNK)
        g = _dot(xb, wg_ref[:, sl])
        u = _dot(xb, wu_ref[:, sl])
        hid = (g * jax.nn.sigmoid(g) * u).astype(BF16)
        part = _dot(hid, wd_ref[sl, :])
        if c == 0:
            acc_ref[...] = part
        else:
            acc_ref[...] += part
    y = ALPHA * x + 0.5 * acc_ref[...]
    o_ref[...] = _layer_norm(y, g_ref[...], b_ref[...])


def _ffn_ln(x, wg, wu, wd, g, b, tm=FFN_ROWS):
    n, d = x.shape
    d_ff = wg.shape[1]
    assert n % tm == 0 and d_ff % FF_CHUNK == 0
    return pl.pallas_call(
        _ffn_ln_kernel,
        grid=(n // tm,),
        in_specs=[
            pl.BlockSpec((tm, d), lambda i: (i, 0)),
            _resident((d, d_ff), lambda i: (0, 0)),
            _resident((d, d_ff), lambda i: (0, 0)),
            _resident((d_ff, d), lambda i: (0, 0)),
            _resident((1, d), lambda i: (0, 0)),
            _resident((1, d), lambda i: (0, 0)),
        ],
        out_specs=pl.BlockSpec((tm, d), lambda i: (i, 0)),
        out_shape=jax.ShapeDtypeStruct((n, d), F32),
        scratch_shapes=[pltpu.VMEM((tm, d), F32)],
        compiler_params=pltpu.CompilerParams(
            dimension_semantics=("arbitrary",), vmem_limit_bytes=VMEM_LIMIT),
        name="ffn_ln",
    )(x, wg, wu, wd, g, b)


CONV_HALO = V7X_SUBLANES
PROJ_PIECE = 256
PIECES_PER_STAGE = 2


def _rglru_coeffs(x, cw_ref, cb_ref, wg_ref, ba_ref, bx_ref, lam_ref, xpad_ref, a_ref, b_ref, between):
    tr = x.shape[0]
    xpad_ref[CONV_HALO:, :] = x
    xc = cb_ref[...] + cw_ref[CONV_WIDTH - 1:CONV_WIDTH, :] * x
    for j in range(CONV_WIDTH - 1):
        back = CONV_WIDTH - 1 - j
        xc = xc + cw_ref[j:j + 1, :] * xpad_ref[CONV_HALO - back:CONV_HALO - back + tr, :]
    xpad_ref[0:CONV_HALO, :] = x[tr - CONV_HALO:, :]

    xcb = xc.astype(BF16)
    sp = jax.nn.softplus(-lam_ref[...])
    for n in range(N_RNN_BLOCKS):
        cs = slice(n * RNN_BLOCK, (n + 1) * RNN_BLOCK)
        g2 = _dot(xcb[:, cs], wg_ref[n])
        r = jax.nn.sigmoid(g2[:, :RNN_BLOCK] + ba_ref[:, cs])
        ig = jax.nn.sigmoid(g2[:, RNN_BLOCK:] + bx_ref[:, cs])
        log_a = -LRU_C * r * sp[:, cs]
        a = jnp.exp(log_a)
        one_minus_a2 = jnp.tanh(-log_a) * (a * a + 1.0)
        a_ref[:, cs] = a
        b_ref[:, cs] = jnp.sqrt(one_minus_a2) * (ig * xc[:, cs])
        between[n]()


def _rglru_scan(a_ref, b_ref, h_ref):
    h = h_ref[...]
    rows = V7X_SUBLANES
    row = lax.broadcasted_iota(I32, (rows, a_ref.shape[1]), 0)
    for g in range(a_ref.shape[0] // rows):
        a = a_ref[g * rows:(g + 1) * rows, :]
        b = b_ref[g * rows:(g + 1) * rows, :]
        d = 1
        while d < rows:
            prev_a = pltpu.roll(a, d, axis=0)
            prev_b = pltpu.roll(b, d, axis=0)
            b = jnp.where(row >= d, a * prev_b + b, b)
            a = jnp.where(row >= d, a * prev_a, a)
            d *= 2
        hs = a * h + b
        a_ref[g * rows:(g + 1) * rows, :] = hs
        h = hs[rows - 1:rows, :]
    h_ref[...] = h


def _in_proj_kernel(h_ref, wqT, wk, wvT, wqiT, wki, wwiT, wxg, wab,
                    bqT, bk, bvT, bqiT, bki, bwiT, bxg, bab,
                    cw_ref, cb_ref, wg_ref, ba_ref, bx_ref, lam_ref,
                    qT_o, k_o, vT_o, qiT_o, ki_o, wiT_o, gab_o, ornn_o,
                    xpad_ref, a_ref, b_ref, hst_ref, gr_ref, *, tk):
    i = pl.program_id(1)

    @pl.when(i == 0)
    def _():
        xpad_ref[0:CONV_HALO, :] = jnp.zeros((CONV_HALO, xpad_ref.shape[1]), F32)
        hst_ref[...] = jnp.zeros_like(hst_ref)

    hb = h_ref[...].astype(BF16)
    tm = hb.shape[0]
    d_rnn = cw_ref.shape[1]
    x_rnn = _dot(hb, wxg[:, :d_rnn]) + bxg[:, :d_rnn]

    def rows_piece(w, b_, out, lo, w_lo=0):
        def run():
            sl = slice(w_lo + lo, w_lo + lo + PROJ_PIECE)
            out[:, lo:lo + PROJ_PIECE] = (_dot(hb, w[:, sl]) + b_[:, sl]).astype(out.dtype)
        return run

    def cols_piece(wT, bT, store, lo, scale=1.0):
        def run():
            sl = slice(lo, lo + PROJ_PIECE)
            store(sl, (_dot_nt(wT[sl, :], hb) + bT[sl, :]) * scale)
        return run

    def store_qT(sl, v):
        qT_o[sl, :] = v.astype(BF16)

    def store_qiT(sl, v):
        qiT_o[sl, :] = v.astype(BF16)

    def store_vT(sl, v):
        for c in range(tm // tk):
            vT_o[c, sl, :] = v[:, c * tk:(c + 1) * tk].astype(BF16)

    def small_pieces():
        ki_o[...] = (_dot(hb, wki[...]) + bki[...]).astype(BF16)
        wiT_o[...] = (_dot_nt(wwiT[...], hb) + bwiT[...]) * (N_IDX_HEADS ** -0.5 * IDX_DIM ** -0.5)

    q_scale = HEAD_DIM ** -0.5 * LOG2E
    pieces = [rows_piece(wk, bk, k_o, lo) for lo in range(0, k_o.shape[1], PROJ_PIECE)]
    pieces += [rows_piece(wab, bab, gab_o, lo) for lo in range(0, gab_o.shape[1], PROJ_PIECE)]
    pieces += [rows_piece(wxg, bxg, gr_ref, lo, d_rnn) for lo in range(0, d_rnn, PROJ_PIECE)]
    pieces += [cols_piece(wqT, bqT, store_qT, lo, q_scale) for lo in range(0, qT_o.shape[0], PROJ_PIECE)]
    pieces += [cols_piece(wqiT, bqiT, store_qiT, lo) for lo in range(0, qiT_o.shape[0], PROJ_PIECE)]
    pieces += [cols_piece(wvT, bvT, store_vT, lo) for lo in range(0, wvT.shape[0], PROJ_PIECE)]
    pieces += [small_pieces]

    def run_pieces(lo, hi):
        def run():
            for piece in pieces[lo:hi]:
                piece()
        return run

    _rglru_coeffs(x_rnn, cw_ref, cb_ref, wg_ref, ba_ref, bx_ref, lam_ref, xpad_ref, a_ref, b_ref,
                  [run_pieces(n * PIECES_PER_STAGE, (n + 1) * PIECES_PER_STAGE) for n in range(N_RNN_BLOCKS)])
    _rglru_scan(a_ref, b_ref, hst_ref)
    ornn_o[...] = (a_ref[...] * jax.nn.gelu(gr_ref[...], approximate=True)).astype(BF16)
    run_pieces(N_RNN_BLOCKS * PIECES_PER_STAGE, len(pieces))()


def _in_proj(h, ws, bs, rnn, batch, seq, tk, tm=PROJ_ROWS):
    n, d = h.shape
    nt = seq // tm
    d_attn = N_HEADS * HEAD_DIM
    d_qi = N_IDX_HEADS * IDX_DIM
    d_rnn = rnn[0].shape[1]
    row = lambda b, i: (b * nt + i, 0)
    col = lambda b, i: (b, 0, i)
    const = lambda b, i: (0, 0)
    in_specs = [pl.BlockSpec((tm, d), row)]
    in_specs += [_resident(w.shape, const) for w in ws]
    in_specs += [_resident(b_.shape, const) for b_ in bs]
    in_specs += [_resident(p.shape, (lambda b, i, nd=p.ndim: (0,) * nd)) for p in rnn]
    out_shape = [
        jax.ShapeDtypeStruct((batch, d_attn, seq), BF16),
        jax.ShapeDtypeStruct((n, d_attn), BF16),
        jax.ShapeDtypeStruct((batch, seq // tk, d_attn, tk), BF16),
        jax.ShapeDtypeStruct((batch, d_qi, seq), BF16),
        jax.ShapeDtypeStruct((n, IDX_DIM), BF16),
        jax.ShapeDtypeStruct((batch, N_IDX_HEADS, seq), F32),
        jax.ShapeDtypeStruct((n, ws[7].shape[1]), F32),
        jax.ShapeDtypeStruct((n, d_rnn), BF16),
    ]
    out_specs = [
        pl.BlockSpec((None, d_attn, tm), col),
        pl.BlockSpec((tm, d_attn), row),
        pl.BlockSpec((None, tm // tk, d_attn, tk), lambda b, i: (b, i, 0, 0)),
        pl.BlockSpec((None, d_qi, tm), col),
        pl.BlockSpec((tm, IDX_DIM), row),
        pl.BlockSpec((None, N_IDX_HEADS, tm), col),
        pl.BlockSpec((tm, ws[7].shape[1]), row),
        pl.BlockSpec((tm, d_rnn), row),
    ]
    return pl.pallas_call(
        functools.partial(_in_proj_kernel, tk=tk),
        grid=(batch, nt),
        in_specs=in_specs,
        out_specs=out_specs,
        out_shape=out_shape,
        scratch_shapes=[pltpu.VMEM((tm + CONV_HALO, d_rnn), F32), pltpu.VMEM((tm, d_rnn), F32),
                        pltpu.VMEM((tm, d_rnn), F32), pltpu.VMEM((1, d_rnn), F32),
                        pltpu.VMEM((tm, d_rnn), F32)],
        compiler_params=pltpu.CompilerParams(
            dimension_semantics=("arbitrary", "arbitrary"), vmem_limit_bytes=VMEM_LIMIT),
        name="in_proj",
    )(h, *ws, *bs, *rnn)


COUNT_ROWS = 64
COUNT_UNKNOWN = 2 ** 30
MAX_EXTRACT = 3
MIN_SEARCH_ITERS = 10
MAX_SEARCH_ITERS = 400
LOG2E = 1.4426950408889634
POS_RADIX = 64
N_SLOPE_PIECES = 3


def _bf16_pieces(c):
    out = []
    for _ in range(N_SLOPE_PIECES):
        piece = float(np.asarray(c, np.float32).astype(BF16).astype(np.float32))
        out.append(piece)
        c = c - piece
    return out


def _dsa_kernel(qT_ref, qiT_ref, wiT_ref, qiTn_ref, wiTn_ref, ki_ref, k_ref, vT_ref, o_ref,
                sc_ref, gmax_ref, l_ref, acc_ref, qaug_ref, s_ref, pos_ref, *, tq, tk, topk):
    i = pl.program_id(1)
    nkb = i + 1
    q_pos = i * tq + lax.broadcasted_iota(I32, (tk, tq), 1)
    key_row = lax.broadcasted_iota(I32, (tk, tq), 0)
    q_row = i * tq + lax.broadcasted_iota(I32, (1, tq), 1)

    def score_block(j, qi_ref, wi_ref, diag_first_query):
        k0 = pl.multiple_of(j * tk, tk)
        ki = ki_ref[pl.ds(k0, tk), :]
        acc = jnp.zeros((tk, tq), F32)
        for h in range(N_IDX_HEADS):
            r = _dot(ki, qi_ref[h * IDX_DIM:(h + 1) * IDX_DIM, :])
            acc = acc + wi_ref[h:h + 1, :] * jnp.maximum(r, 0.0)
        if diag_first_query is not None:
            acc = jnp.where(k0 + key_row <= diag_first_query + (q_pos - i * tq), acc, NEG_INF)
        sc_ref[j] = acc
        gmax_ref[...] = jnp.maximum(gmax_ref[...], acc)

    def score_next(j):
        score_block(j, qiTn_ref, wiTn_ref, None)

    @pl.when(i == 0)
    def _():
        gmax_ref[...] = jnp.full(gmax_ref.shape, -jnp.inf, F32)
        score_block(0, qiT_ref, wiT_ref, 0)

    n_pairs_all = (nkb + 1) // 2

    @pl.when(nkb % 2 == 1)
    def _():
        sc_ref[nkb] = jnp.full((tk, tq), NEG_INF, F32)

    def count_where(pred):
        def body(p, acc):
            for j in (2 * p, 2 * p + 1):
                m = jnp.where(pred(sc_ref[j], j), 1, 0).astype(I32)
                for r in range(tk // COUNT_ROWS):
                    acc = acc + m[r * COUNT_ROWS:(r + 1) * COUNT_ROWS, :]
            return acc
        acc = lax.fori_loop(0, n_pairs_all, body, jnp.zeros((COUNT_ROWS, tq), I32))
        return jnp.sum(acc, axis=0, keepdims=True)

    def count_ge(v):
        return count_where(lambda x, j: x >= v)

    gmax = gmax_ref[...]
    lo = jnp.min(gmax, axis=0, keepdims=True)
    ub = jnp.max(gmax, axis=0, keepdims=True)
    hi = ub + jnp.maximum(jnp.abs(ub) * 2.0 ** -20, 1e-30)
    all_rows = q_row + 1 <= topk
    lo = jnp.where(all_rows, NEG_INF, lo)
    c_lo = jnp.where(all_rows, topk, COUNT_UNKNOWN)
    c_hi = jnp.zeros((1, tq), I32)
    done = (c_lo == topk).astype(I32)

    def search_step(st):
        it, lo, hi, c_lo, c_hi, done = st
        v = 0.5 * lo + 0.5 * hi
        adjacent = jnp.logical_or(v <= lo, v >= hi)
        c = count_ge(v)
        live = jnp.logical_and(done == 0, jnp.logical_not(adjacent))
        up = jnp.logical_and(live, c >= topk)
        dn = jnp.logical_and(live, c < topk)
        lo = jnp.where(up, v, lo)
        c_lo = jnp.where(up, c, c_lo)
        hi = jnp.where(dn, v, hi)
        c_hi = jnp.where(dn, c, c_hi)
        done = jnp.where(jnp.logical_or(adjacent, c_lo == topk), 1, done)
        return it + 1, lo, hi, c_lo, c_hi, done

    def all_done(st):
        return jnp.min(st[5]) == 1

    def near_top(st):
        it, lo, hi, c_lo, c_hi, done = st
        return jnp.min(jnp.where(jnp.logical_or(done == 1, topk - c_hi <= MAX_EXTRACT), 1, 0)) == 1

    def keep_bisecting(stop):
        return lambda st: jnp.logical_and(st[0] < MAX_SEARCH_ITERS, jnp.logical_not(stop(st)))

    st = (jnp.int32(0), lo, hi, c_lo, c_hi, done)
    st = lax.fori_loop(0, MIN_SEARCH_ITERS, lambda t, st: search_step(st), st)
    st = lax.while_loop(keep_bisecting(near_top), search_step, st)
    it, lo, hi, c_lo, c_hi, done = st

    def step_down(t, sd):
        hi_s, c_s = sd
        def body(p, acc):
            for j in (2 * p, 2 * p + 1):
                x = jnp.where(sc_ref[j] < hi_s, sc_ref[j], -jnp.inf)
                for r in range(tk // COUNT_ROWS):
                    acc = jnp.maximum(acc, x[r * COUNT_ROWS:(r + 1) * COUNT_ROWS, :])
            return acc
        acc = lax.fori_loop(0, n_pairs_all, body, jnp.full((COUNT_ROWS, tq), -jnp.inf, F32))
        below = jnp.max(acc, axis=0, keepdims=True)
        move = jnp.logical_and(done == 0, c_s < topk)
        return jnp.where(move, below, hi_s), jnp.where(move, c_s + 1, c_s)

    hi_s, c_s = lax.fori_loop(0, MAX_EXTRACT, step_down, (hi, c_hi))
    stepped = jnp.logical_and(done == 0, c_s == topk)
    c_chk = count_ge(jnp.where(stepped, hi_s, lo))
    lo = jnp.where(stepped, hi_s, lo)
    c_lo = jnp.where(stepped, c_chk, c_lo)
    done = jnp.where(c_lo == topk, 1, done)
    _, thr, _, n_ge, _, _ = lax.while_loop(keep_bisecting(all_done), search_step,
                                           (it, lo, hi, c_lo, c_hi, done))

    @pl.when(jnp.max(n_ge) > topk)
    def _():
        need = topk - count_where(lambda x, j: x > thr)

        idx_bits = (sc_ref.shape[0] * tk - 1).bit_length()

        def idx_step(t, c):
            cand = c | jnp.left_shift(jnp.int32(1), idx_bits - 1 - t)
            below = count_where(lambda x, j: jnp.logical_and(x == thr, j * tk + key_row < cand))
            return jnp.where(below < need, cand, c)

        cut = lax.fori_loop(0, idx_bits, idx_step, jnp.zeros((1, tq), I32))

        def demote_block(j, carry):
            x = sc_ref[j]
            drop = jnp.logical_and(x == thr, j * tk + key_row > cut)
            sc_ref[j] = jnp.where(drop, NEG_INF, x)
            return carry

        lax.fori_loop(0, nkb, demote_block, 0)

    slope_pieces = [_bf16_pieces(2.0 ** (-8.0 * (h + 1) / N_HEADS) * LOG2E) for h in range(N_HEADS)]
    slope_l2 = [sum(pieces) for pieces in slope_pieces]
    row_a = lax.broadcasted_iota(I32, (HEAD_DIM, tq), 0)
    for h in range(N_HEADS):
        hs = slice(h * HEAD_DIM, (h + 1) * HEAD_DIM)
        aug = jnp.zeros((HEAD_DIM, tq), F32)
        for r, c in enumerate(slope_pieces[h]):
            aug = jnp.where(row_a == 2 * r, c * POS_RADIX, jnp.where(row_a == 2 * r + 1, c, aug))
        qaug_ref[h, 0:HEAD_DIM, :] = qT_ref[hs, :]
        qaug_ref[h, HEAD_DIM:, :] = aug.astype(BF16)
    lane_p = lax.broadcasted_iota(I32, (tk, HEAD_DIM), 1)
    row_p = lax.broadcasted_iota(I32, (tk, HEAD_DIM), 0)
    pos = jnp.where(lane_p % 2 == 0, row_p // POS_RADIX, row_p % POS_RADIX)
    pos_ref[...] = jnp.where(lane_p < 2 * N_SLOPE_PIECES, pos, 0).astype(F32).astype(BF16)

    l_ref[...] = jnp.zeros(l_ref.shape, F32)
    acc_ref[...] = jnp.zeros(acc_ref.shape, F32)

    key_minus_query = key_row - (q_pos - i * tq)

    def logits_block(j, scores, m_run, slot):
        k0 = pl.multiple_of(j * tk, tk)
        k0f = jnp.asarray(j * tk).astype(F32)
        selected = jnp.logical_and(scores >= thr, key_minus_query <= i * tq - j * tk)
        bias = jnp.where(selected, 0.0, NEG_INF)
        m_blk = []
        for h in range(N_HEADS):
            hs = slice(h * HEAD_DIM, (h + 1) * HEAD_DIM)
            lhs = jnp.concatenate([k_ref[pl.ds(k0, tk), hs], pos_ref[...]], axis=1)
            s = _dot(lhs, qaug_ref[h]) + bias
            s_ref[slot, h] = s
            m_blk.append(jnp.max(s, axis=0, keepdims=True) + k0f * slope_l2[h])
        m_new = jnp.maximum(m_run, jnp.concatenate(m_blk, axis=0))
        return m_new, jnp.exp2(m_run - m_new)

    def values_block(j, m_j, alpha, slot):
        sums = []
        for h in range(N_HEADS):
            hs = slice(h * HEAD_DIM, (h + 1) * HEAD_DIM)
            m_local = m_j[h:h + 1, :] - jnp.asarray(j * tk).astype(F32) * slope_l2[h]
            p = jnp.exp2(s_ref[slot, h] - m_local)
            sums.append(jnp.sum(p, axis=0, keepdims=True))
            pv = _dot(vT_ref[j, hs, :], p.astype(BF16))
            acc_ref[hs, :] = alpha[h:h + 1, :] * acc_ref[hs, :] + pv
        l_ref[...] = alpha * l_ref[...] + jnp.concatenate(sums, axis=0)

    def attn_step(j, carry, slot, scores_next):
        m_j, alpha_j = carry
        nxt = logits_block(j + 1, scores_next, m_j, 1 - slot)
        values_block(j, m_j, alpha_j, slot)
        return nxt

    def attn_pair(p, carry):
        scores_a = sc_ref[2 * p + 1]
        scores_b = sc_ref[2 * p + 2]
        carry = attn_step(2 * p, carry, 0, scores_a)
        score_next(2 * p)
        carry = attn_step(2 * p + 1, carry, 1, scores_b)
        score_next(2 * p + 1)
        return carry

    gmax_ref[...] = jnp.full(gmax_ref.shape, -jnp.inf, F32)
    first = logits_block(0, sc_ref[0], jnp.full((N_HEADS, tq), -jnp.inf, F32), 0)
    n_pairs = (nkb - 1) // 2
    m_c, alpha_c = lax.fori_loop(0, n_pairs, attn_pair, first)
    j_rest = 2 * n_pairs

    @pl.when(j_rest == nkb - 1)
    def _():
        values_block(j_rest, m_c, alpha_c, 0)
        score_next(j_rest)

    @pl.when(j_rest < nkb - 1)
    def _():
        m_l, alpha_l = attn_step(j_rest, (m_c, alpha_c), 0, sc_ref[j_rest + 1])
        values_block(j_rest + 1, m_l, alpha_l, 1)
        score_next(j_rest)
        score_next(j_rest + 1)

    @pl.when(i + 1 < pl.num_programs(1))
    def _():
        score_block(nkb, qiTn_ref, wiTn_ref, (i + 1) * tq)

    inv_l = 1.0 / l_ref[...]
    for h in range(N_HEADS):
        hs = slice(h * HEAD_DIM, (h + 1) * HEAD_DIM)
        o_ref[:, hs] = (acc_ref[hs, :] * inv_l[h:h + 1, :]).T.astype(BF16)


def _dsa(qT, qiT, wiT, ki, k, vT, batch, seq, tq, tk):
    d_attn = qT.shape[1]
    nq = seq // tq
    topk = min(TOPK_MAX, seq // 4)
    assert tk >= topk and tk % COUNT_ROWS == 0 and seq % tk == 0 and tq == tk
    col = lambda b, i: (b, 0, i)
    col_next = lambda b, i: (b, 0, jnp.minimum(i + 1, nq - 1))
    kernel = functools.partial(_dsa_kernel, tq=tq, tk=tk, topk=topk)
    return pl.pallas_call(
        kernel,
        grid=(batch, nq),
        in_specs=[
            pl.BlockSpec((None, d_attn, tq), col),
            pl.BlockSpec((None, qiT.shape[1], tq), col),
            pl.BlockSpec((None, N_IDX_HEADS, tq), col),
            pl.BlockSpec((None, qiT.shape[1], tq), col_next),
            pl.BlockSpec((None, N_IDX_HEADS, tq), col_next),
            _resident((seq, IDX_DIM), lambda b, i: (b, 0)),
            _resident((seq, d_attn), lambda b, i: (b, 0)),
            _resident((None, seq // tk, d_attn, tk), lambda b, i: (b, 0, 0, 0)),
        ],
        out_specs=pl.BlockSpec((tq, d_attn), lambda b, i: (b * nq + i, 0)),
        out_shape=jax.ShapeDtypeStruct((batch * seq, d_attn), BF16),
        scratch_shapes=[
            pltpu.VMEM((seq // tk, tk, tq), F32),
            pltpu.VMEM((tk, tq), F32),
            pltpu.VMEM((N_HEADS, tq), F32),
            pltpu.VMEM((d_attn, tq), F32),
            pltpu.VMEM((N_HEADS, 2 * HEAD_DIM, tq), BF16),
            pltpu.VMEM((2, N_HEADS, tk, tq), F32),
            pltpu.VMEM((tk, HEAD_DIM), BF16),
        ],
        compiler_params=pltpu.CompilerParams(
            dimension_semantics=("arbitrary", "arbitrary"), vmem_limit_bytes=VMEM_LIMIT),
        name="dsa",
    )(qT, qiT, wiT, qiT, wiT, ki, k, vT)


def _merge_ln_kernel(h_ref, oa_ref, or_ref, ga_ref, gb_ref, wpa, wpr, wout, g_ref, b_ref, o_ref):
    ya = _dot(oa_ref[...], wpa[...])
    yr = _dot(or_ref[...], wpr[...])
    y = jax.nn.sigmoid(ga_ref[...]) * ya + jax.nn.sigmoid(gb_ref[...]) * yr
    mix = _dot(y.astype(BF16), wout[...])
    o_ref[...] = _layer_norm(ALPHA * h_ref[...] + mix, g_ref[...], b_ref[...])


def _merge_ln(h, o_attn, o_rnn, gab, wpa, wpr, wout, g, b, tm=FFN_ROWS):
    n, d = h.shape
    const = lambda i: (0, 0)
    return pl.pallas_call(
        _merge_ln_kernel,
        grid=(n // tm,),
        in_specs=[
            pl.BlockSpec((tm, d), lambda i: (i, 0)),
            pl.BlockSpec((tm, o_attn.shape[1]), lambda i: (i, 0)),
            pl.BlockSpec((tm, o_rnn.shape[1]), lambda i: (i, 0)),
            pl.BlockSpec((tm, d), lambda i: (i, 0)),
            pl.BlockSpec((tm, d), lambda i: (i, 1)),
            _resident(wpa.shape, const),
            _resident(wpr.shape, const),
            _resident(wout.shape, const),
            _resident((1, d), const),
            _resident((1, d), const),
        ],
        out_specs=pl.BlockSpec((tm, d), lambda i: (i, 0)),
        out_shape=jax.ShapeDtypeStruct((n, d), F32),
        compiler_params=pltpu.CompilerParams(
            dimension_semantics=("arbitrary",), vmem_limit_bytes=VMEM_LIMIT),
        name="merge_ln",
    )(h, o_attn, o_rnn, gab, gab, wpa, wpr, wout, g, b)


def _layer(x2, batch, seq, p, tq=DSA_BLOCK, tk=DSA_BLOCK):
    d_attn = N_HEADS * HEAD_DIM
    row = lambda a: a.reshape(1, -1).astype(F32)
    colv = lambda a: a.reshape(-1, 1).astype(F32)

    h1 = _ffn_ln(x2, p['ffn1_w_gate'].astype(BF16), p['ffn1_w_up'].astype(BF16),
                 p['ffn1_w_down'].astype(BF16), row(p['ln1_g']), row(p['ln1_b']))

    w_in, b_in = p['w_in'].astype(BF16), p['b_in']
    o = 0
    wq, bq = w_in[:, o:o + d_attn], b_in[o:o + d_attn]; o += d_attn
    wk, bk = w_in[:, o:o + d_attn], b_in[o:o + d_attn]; o += d_attn
    wv, bv = w_in[:, o:o + d_attn], b_in[o:o + d_attn]; o += d_attn
    n_qi = N_IDX_HEADS * IDX_DIM
    wqi, bqi = w_in[:, o:o + n_qi], b_in[o:o + n_qi]; o += n_qi
    wki, bki = w_in[:, o:o + IDX_DIM], b_in[o:o + IDX_DIM]; o += IDX_DIM
    wwi, bwi = w_in[:, o:o + N_IDX_HEADS], b_in[o:o + N_IDX_HEADS]; o += N_IDX_HEADS
    d_rnn = N_RNN_BLOCKS * RNN_BLOCK
    wxg, bxg = w_in[:, o:o + 2 * d_rnn], b_in[o:o + 2 * d_rnn]; o += 2 * d_rnn
    wab, bab = w_in[:, o:], b_in[o:]
    ws = [w.astype(BF16) for w in (wq.T, wk, wv.T, wqi.T, wki, wwi.T, wxg, wab)]
    bs = [colv(bq), row(bk), colv(bv), colv(bqi), row(bki), colv(bwi), row(bxg), row(bab)]
    w_gates = jnp.concatenate([p['lru_w_a'], p['lru_w_x']], axis=-1).astype(BF16)
    rnn = [p['conv_w'].astype(F32), row(p['conv_b']), w_gates, row(p['lru_b_a']), row(p['lru_b_x']),
           row(p['lru_lambda'])]
    qT, k, vT, qiT, ki, wiT, gab, o_rnn = _in_proj(h1, ws, bs, rnn, batch, seq, tk)

    o_attn = _dsa(qT, qiT, wiT, ki, k, vT, batch, seq, tq, tk)

    h2 = _merge_ln(h1, o_attn, o_rnn, gab, p['w_proj_attn'].astype(BF16),
                   p['w_proj_rnn'].astype(BF16), p['w_out'].astype(BF16),
                   row(p['ln2_g']), row(p['ln2_b']))

    return _ffn_ln(h2, p['ffn2_w_gate'].astype(BF16), p['ffn2_w_up'].astype(BF16),
                   p['ffn2_w_down'].astype(BF16), row(p['ln3_g']), row(p['ln3_b']))


_PARAM_NAMES = ('ln1_g', 'ln1_b', 'ffn1_w_gate', 'ffn1_w_up', 'ffn1_w_down', 'w_in', 'b_in', 'conv_w',
                'conv_b', 'lru_w_a', 'lru_b_a', 'lru_w_x', 'lru_b_x', 'lru_lambda', 'w_proj_attn',
                'w_proj_rnn', 'w_out', 'ln2_g', 'ln2_b', 'ffn2_w_gate', 'ffn2_w_up', 'ffn2_w_down',
                'ln3_g', 'ln3_b')


@jax.jit
def _forward(x, *params):
    batch, seq, d = x.shape
    h = x.reshape(batch * seq, d)
    for l in range(DEPTH):
        p = {name: a[l] for name, a in zip(_PARAM_NAMES, params)}
        h = _layer(h, batch, seq, p)
    return h.reshape(batch, seq, d)


def kernel(x, ln1_g, ln1_b, ffn1_w_gate, ffn1_w_up, ffn1_w_down, w_in, b_in, conv_w, conv_b, lru_w_a, lru_b_a, lru_w_x, lru_b_x, lru_lambda, w_proj_attn, w_proj_rnn, w_out, ln2_g, ln2_b, ffn2_w_gate, ffn2_w_up, ffn2_w_down, ln3_g, ln3_b):
    return _forward(x, ln1_g, ln1_b, ffn1_w_gate, ffn1_w_up, ffn1_w_down, w_in, b_in, conv_w, conv_b,
                    lru_w_a, lru_b_a, lru_w_x, lru_b_x, lru_lambda, w_proj_attn, w_proj_rnn, w_out,
                    ln2_g, ln2_b, ffn2_w_gate, ffn2_w_up, ffn2_w_down, ln3_g, ln3_b)
```

```python
import functools

import jax
import jax.numpy as jnp
import numpy as np
from jax import lax
from jax.experimental import pallas as pl
from jax.experimental.pallas import tpu as pltpu

F32 = jnp.float32
BF16 = jnp.bfloat16
I32 = jnp.int32

N_HEADS = 8
HEAD_DIM = 128
N_IDX_HEADS = 16
IDX_DIM = 64
TOPK_MAX = 256
N_RNN_BLOCKS = 8
RNN_BLOCK = 128
CONV_WIDTH = 4
LRU_C = 8.0
LN_EPS = 1e-5
DEPTH = 1
ALPHA = (2.0 * DEPTH) ** 0.25
NEG_INF = -1e30

V7X_VMEM_BYTES = 64 * 1024 * 1024
V7X_SUBLANES = 8
V7X_LANES = 128
VMEM_LIMIT = V7X_VMEM_BYTES // 8 * 7
FFN_ROWS = 512
PROJ_ROWS = 256
DSA_BLOCK = 256


def _resident(shape, index_map):
    return pl.BlockSpec(shape, index_map, pipeline_mode=pl.Buffered(1))


def _layer_norm(y, g, b):
    mu = jnp.mean(y, axis=-1, keepdims=True)
    d = y - mu
    var = jnp.mean(d * d, axis=-1, keepdims=True)
    return d * lax.rsqrt(var + LN_EPS) * g + b


def _dot(a, b):
    return jnp.dot(a, b, preferred_element_type=F32)


def _dot_nt(a, b):
    return lax.dot_general(a, b, (((1,), (1,)), ((), ())), preferred_element_type=F32)


FF_CHUNK = 256


def _ffn_ln_kernel(x_ref, wg_ref, wu_ref, wd_ref, g_ref, b_ref, o_ref, acc_ref):
    x = x_ref[...]
    xb = x.astype(BF16)
    d_ff = wg_ref.shape[1]
    for c in range(d_ff // FF_CHUNK):
        sl = slice(c * FF_CHUNK, (c + 1) * FF_CHUNK)
        g = _dot(xb, wg_ref[:, sl])
        u = _dot(xb, wu_ref[:, sl])
        hid = (g * jax.nn.sigmoid(g) * u).astype(BF16)
        part = _dot(hid, wd_ref[sl, :])
        if c == 0:
            acc_ref[...] = part
        else:
            acc_ref[...] += part
    y = ALPHA * x + 0.5 * acc_ref[...]
    o_ref[...] = _layer_norm(y, g_ref[...], b_ref[...])


def _ffn_ln(x, wg, wu, wd, g, b, tm=FFN_ROWS):
    n, d = x.shape
    d_ff = wg.shape[1]
    assert n % tm == 0 and d_ff % FF_CHUNK == 0
    return pl.pallas_call(
        _ffn_ln_kernel,
        grid=(n // tm,),
        in_specs=[
            pl.BlockSpec((tm, d), lambda i: (i, 0)),
            _resident((d, d_ff), lambda i: (0, 0)),
            _resident((d, d_ff), lambda i: (0, 0)),
            _resident((d_ff, d), lambda i: (0, 0)),
            _resident((1, d), lambda i: (0, 0)),
            _resident((1, d), lambda i: (0, 0)),
        ],
        out_specs=pl.BlockSpec((tm, d), lambda i: (i, 0)),
        out_shape=jax.ShapeDtypeStruct((n, d), F32),
        scratch_shapes=[pltpu.VMEM((tm, d), F32)],
        compiler_params=pltpu.CompilerParams(
            dimension_semantics=("arbitrary",), vmem_limit_bytes=VMEM_LIMIT),
        name="ffn_ln",
    )(x, wg, wu, wd, g, b)


CONV_HALO = V7X_SUBLANES
PROJ_PIECE = 256
PIECES_PER_STAGE = 2


def _rglru_coeffs(x, cw_ref, cb_ref, wg_ref, ba_ref, bx_ref, lam_ref, xpad_ref, a_ref, b_ref, between):
    tr = x.shape[0]
    xpad_ref[CONV_HALO:, :] = x
    xc = cb_ref[...] + cw_ref[CONV_WIDTH - 1:CONV_WIDTH, :] * x
    for j in range(CONV_WIDTH - 1):
        back = CONV_WIDTH - 1 - j
        xc = xc + cw_ref[j:j + 1, :] * xpad_ref[CONV_HALO - back:CONV_HALO - back + tr, :]
    xpad_ref[0:CONV_HALO, :] = x[tr - CONV_HALO:, :]

    xcb = xc.astype(BF16)
    sp = jax.nn.softplus(-lam_ref[...])
    for n in range(N_RNN_BLOCKS):
        cs = slice(n * RNN_BLOCK, (n + 1) * RNN_BLOCK)
        g2 = _dot(xcb[:, cs], wg_ref[n])
        r = jax.nn.sigmoid(g2[:, :RNN_BLOCK] + ba_ref[:, cs])
        ig = jax.nn.sigmoid(g2[:, RNN_BLOCK:] + bx_ref[:, cs])
        log_a = -LRU_C * r * sp[:, cs]
        a = jnp.exp(log_a)
        one_minus_a2 = jnp.tanh(-log_a) * (a * a + 1.0)
        a_ref[:, cs] = a
        b_ref[:, cs] = jnp.sqrt(one_minus_a2) * (ig * xc[:, cs])
        between[n]()


def _rglru_scan(a_ref, b_ref, h_ref):
    h = h_ref[...]
    rows = V7X_SUBLANES
    row = lax.broadcasted_iota(I32, (rows, a_ref.shape[1]), 0)
    for g in range(a_ref.shape[0] // rows):
        a = a_ref[g * rows:(g + 1) * rows, :]
        b = b_ref[g * rows:(g + 1) * rows, :]
        d = 1
        while d < rows:
            prev_a = pltpu.roll(a, d, axis=0)
            prev_b = pltpu.roll(b, d, axis=0)
            b = jnp.where(row >= d, a * prev_b + b, b)
            a = jnp.where(row >= d, a * prev_a, a)
            d *= 2
        hs = a * h + b
        a_ref[g * rows:(g + 1) * rows, :] = hs
        h = hs[rows - 1:rows, :]
    h_ref[...] = h


def _in_proj_kernel(h_ref, wqT, wk, wvT, wqiT, wki, wwiT, wxg, wab,
                    bqT, bk, bvT, bqiT, bki, bwiT, bxg, bab,
                    cw_ref, cb_ref, wg_ref, ba_ref, bx_ref, lam_ref,
                    qT_o, k_o, vT_o, qiT_o, ki_o, wiT_o, gab_o, ornn_o,
                    xpad_ref, a_ref, b_ref, hst_ref, gr_ref, *, tk):
    i = pl.program_id(1)

    @pl.when(i == 0)
    def _():
        xpad_ref[0:CONV_HALO, :] = jnp.zeros((CONV_HALO, xpad_ref.shape[1]), F32)
        hst_ref[...] = jnp.zeros_like(hst_ref)

    hb = h_ref[...].astype(BF16)
    tm = hb.shape[0]
    d_rnn = cw_ref.shape[1]
    x_rnn = _dot(hb, wxg[:, :d_rnn]) + bxg[:, :d_rnn]

    def rows_piece(w, b_, out, lo, w_lo=0):
        def run():
            sl = slice(w_lo + lo, w_lo + lo + PROJ_PIECE)
            out[:, lo:lo + PROJ_PIECE] = (_dot(hb, w[:, sl]) + b_[:, sl]).astype(out.dtype)
        return run

    def cols_piece(wT, bT, store, lo, scale=1.0):
        def run():
            sl = slice(lo, lo + PROJ_PIECE)
            store(sl, (_dot_nt(wT[sl, :], hb) + bT[sl, :]) * scale)
        return run

    def store_qT(sl, v):
        qT_o[sl, :] = v.astype(BF16)

    def store_qiT(sl, v):
        qiT_o[sl, :] = v.astype(BF16)

    def store_vT(sl, v):
        for c in range(tm // tk):
            vT_o[c, sl, :] = v[:, c * tk:(c + 1) * tk].astype(BF16)

    def small_pieces():
        ki_o[...] = (_dot(hb, wki[...]) + bki[...]).astype(BF16)
        wiT_o[...] = (_dot_nt(wwiT[...], hb) + bwiT[...]) * (N_IDX_HEADS ** -0.5 * IDX_DIM ** -0.5)

    q_scale = HEAD_DIM ** -0.5 * LOG2E
    pieces = [rows_piece(wk, bk, k_o, lo) for lo in range(0, k_o.shape[1], PROJ_PIECE)]
    pieces += [rows_piece(wab, bab, gab_o, lo) for lo in range(0, gab_o.shape[1], PROJ_PIECE)]
    pieces += [rows_piece(wxg, bxg, gr_ref, lo, d_rnn) for lo in range(0, d_rnn, PROJ_PIECE)]
    pieces += [cols_piece(wqT, bqT, store_qT, lo, q_scale) for lo in range(0, qT_o.shape[0], PROJ_PIECE)]
    pieces += [cols_piece(wqiT, bqiT, store_qiT, lo) for lo in range(0, qiT_o.shape[0], PROJ_PIECE)]
    pieces += [cols_piece(wvT, bvT, store_vT, lo) for lo in range(0, wvT.shape[0], PROJ_PIECE)]
    pieces += [small_pieces]

    def run_pieces(lo, hi):
        def run():
            for piece in pieces[lo:hi]:
                piece()
        return run

    _rglru_coeffs(x_rnn, cw_ref, cb_ref, wg_ref, ba_ref, bx_ref, lam_ref, xpad_ref, a_ref, b_ref,
                  [run_pieces(n * PIECES_PER_STAGE, (n + 1) * PIECES_PER_STAGE) for n in range(N_RNN_BLOCKS)])
    _rglru_scan(a_ref, b_ref, hst_ref)
    ornn_o[...] = (a_ref[...] * jax.nn.gelu(gr_ref[...], approximate=True)).astype(BF16)
    run_pieces(N_RNN_BLOCKS * PIECES_PER_STAGE, len(pieces))()


def _in_proj(h, ws, bs, rnn, batch, seq, tk, tm=PROJ_ROWS):
    n, d = h.shape
    nt = seq // tm
    d_attn = N_HEADS * HEAD_DIM
    d_qi = N_IDX_HEADS * IDX_DIM
    d_rnn = rnn[0].shape[1]
    row = lambda b, i: (b * nt + i, 0)
    col = lambda b, i: (b, 0, i)
    const = lambda b, i: (0, 0)
    in_specs = [pl.BlockSpec((tm, d), row)]
    in_specs += [_resident(w.shape, const) for w in ws]
    in_specs += [_resident(b_.shape, const) for b_ in bs]
    in_specs += [_resident(p.shape, (lambda b, i, nd=p.ndim: (0,) * nd)) for p in rnn]
    out_shape = [
        jax.ShapeDtypeStruct((batch, d_attn, seq), BF16),
        jax.ShapeDtypeStruct((n, d_attn), BF16),
        jax.ShapeDtypeStruct((batch, seq // tk, d_attn, tk), BF16),
        jax.ShapeDtypeStruct((batch, d_qi, seq), BF16),
        jax.ShapeDtypeStruct((n, IDX_DIM), BF16),
        jax.ShapeDtypeStruct((batch, N_IDX_HEADS, seq), F32),
        jax.ShapeDtypeStruct((n, ws[7].shape[1]), F32),
        jax.ShapeDtypeStruct((n, d_rnn), BF16),
    ]
    out_specs = [
        pl.BlockSpec((None, d_attn, tm), col),
        pl.BlockSpec((tm, d_attn), row),
        pl.BlockSpec((None, tm // tk, d_attn, tk), lambda b, i: (b, i, 0, 0)),
        pl.BlockSpec((None, d_qi, tm), col),
        pl.BlockSpec((tm, IDX_DIM), row),
        pl.BlockSpec((None, N_IDX_HEADS, tm), col),
        pl.BlockSpec((tm, ws[7].shape[1]), row),
        pl.BlockSpec((tm, d_rnn), row),
    ]
    return pl.pallas_call(
        functools.partial(_in_proj_kernel, tk=tk),
        grid=(batch, nt),
        in_specs=in_specs,
        out_specs=out_specs,
        out_shape=out_shape,
        scratch_shapes=[pltpu.VMEM((tm + CONV_HALO, d_rnn), F32), pltpu.VMEM((tm, d_rnn), F32),
                        pltpu.VMEM((tm, d_rnn), F32), pltpu.VMEM((1, d_rnn), F32),
                        pltpu.VMEM((tm, d_rnn), F32)],
        compiler_params=pltpu.CompilerParams(
            dimension_semantics=("arbitrary", "arbitrary"), vmem_limit_bytes=VMEM_LIMIT),
        name="in_proj",
    )(h, *ws, *bs, *rnn)


COUNT_ROWS = 32
COUNT_UNKNOWN = 2 ** 30
MAX_EXTRACT = 3
MIN_SEARCH_ITERS = 10
MAX_SEARCH_ITERS = 400
LOG2E = 1.4426950408889634
POS_RADIX = 64
N_SLOPE_PIECES = 3


def _bf16_pieces(c):
    out = []
    for _ in range(N_SLOPE_PIECES):
        piece = float(np.asarray(c, np.float32).astype(BF16).astype(np.float32))
        out.append(piece)
        c = c - piece
    return out


def _dsa_kernel(qT_ref, qiT_ref, wiT_ref, qiTn_ref, wiTn_ref, ki_ref, k_ref, vT_ref, o_ref,
                sc_ref, gmax_ref, l_ref, acc_ref, qaug_ref, s_ref, pos_ref, *, tq, tk, topk):
    i = pl.program_id(1)
    nkb = i + 1
    q_pos = i * tq + lax.broadcasted_iota(I32, (tk, tq), 1)
    key_row = lax.broadcasted_iota(I32, (tk, tq), 0)
    q_row = i * tq + lax.broadcasted_iota(I32, (1, tq), 1)

    def score_block(j, qi_ref, wi_ref, diag_first_query):
        k0 = pl.multiple_of(j * tk, tk)
        ki = ki_ref[pl.ds(k0, tk), :]
        acc = jnp.zeros((tk, tq), F32)
        for h in range(N_IDX_HEADS):
            r = _dot(ki, qi_ref[h * IDX_DIM:(h + 1) * IDX_DIM, :])
            acc = acc + wi_ref[h:h + 1, :] * jnp.maximum(r, 0.0)
        if diag_first_query is not None:
            acc = jnp.where(k0 + key_row <= diag_first_query + (q_pos - i * tq), acc, NEG_INF)
        sc_ref[j] = acc
        gmax_ref[...] = jnp.maximum(gmax_ref[...], acc)

    def score_next(j):
        score_block(j, qiTn_ref, wiTn_ref, None)

    @pl.when(i == 0)
    def _():
        gmax_ref[...] = jnp.full(gmax_ref.shape, -jnp.inf, F32)
        score_block(0, qiT_ref, wiT_ref, 0)

    n_pairs_all = (nkb + 1) // 2

    @pl.when(nkb % 2 == 1)
    def _():
        sc_ref[nkb] = jnp.full((tk, tq), NEG_INF, F32)

    def count_where(pred):
        def body(p, acc):
            for j in (2 * p, 2 * p + 1):
                m = jnp.where(pred(sc_ref[j], j), 1, 0).astype(I32)
                for r in range(tk // COUNT_ROWS):
                    acc = acc + m[r * COUNT_ROWS:(r + 1) * COUNT_ROWS, :]
            return acc
        acc = lax.fori_loop(0, n_pairs_all, body, jnp.zeros((COUNT_ROWS, tq), I32))
        return jnp.sum(acc, axis=0, keepdims=True)

    def count_ge(v):
        return count_where(lambda x, j: x >= v)

    gmax = gmax_ref[...]
    lo = jnp.min(gmax, axis=0, keepdims=True)
    ub = jnp.max(gmax, axis=0, keepdims=True)
    hi = ub + jnp.maximum(jnp.abs(ub) * 2.0 ** -20, 1e-30)
    all_rows = q_row + 1 <= topk
    lo = jnp.where(all_rows, NEG_INF, lo)
    c_lo = jnp.where(all_rows, topk, COUNT_UNKNOWN)
    c_hi = jnp.zeros((1, tq), I32)
    done = (c_lo == topk).astype(I32)

    def search_step(st):
        it, lo, hi, c_lo, c_hi, done = st
        v = 0.5 * lo + 0.5 * hi
        adjacent = jnp.logical_or(v <= lo, v >= hi)
        c = count_ge(v)
        live = jnp.logical_and(done == 0, jnp.logical_not(adjacent))
        up = jnp.logical_and(live, c >= topk)
        dn = jnp.logical_and(live, c < topk)
        lo = jnp.where(up, v, lo)
        c_lo = jnp.where(up, c, c_lo)
        hi = jnp.where(dn, v, hi)
        c_hi = jnp.where(dn, c, c_hi)
        done = jnp.where(jnp.logical_or(adjacent, c_lo == topk), 1, done)
        return it + 1, lo, hi, c_lo, c_hi, done

    def all_done(st):
        return jnp.min(st[5]) == 1

    def near_top(st):
        it, lo, hi, c_lo, c_hi, done = st
        return jnp.min(jnp.where(jnp.logical_or(done == 1, topk - c_hi <= MAX_EXTRACT), 1, 0)) == 1

    def keep_bisecting(stop):
        return lambda st: jnp.logical_and(st[0] < MAX_SEARCH_ITERS, jnp.logical_not(stop(st)))

    st = (jnp.int32(0), lo, hi, c_lo, c_hi, done)
    st = lax.fori_loop(0, MIN_SEARCH_ITERS, lambda t, st: search_step(st), st)
    st = lax.while_loop(keep_bisecting(near_top), search_step, st)
    it, lo, hi, c_lo, c_hi, done = st

    def step_down(t, sd):
        hi_s, c_s = sd
        def body(p, acc):
            for j in (2 * p, 2 * p + 1):
                x = jnp.where(sc_ref[j] < hi_s, sc_ref[j], -jnp.inf)
                for r in range(tk // COUNT_ROWS):
                    acc = jnp.maximum(acc, x[r * COUNT_ROWS:(r + 1) * COUNT_ROWS, :])
            return acc
        acc = lax.fori_loop(0, n_pairs_all, body, jnp.full((COUNT_ROWS, tq), -jnp.inf, F32))
        below = jnp.max(acc, axis=0, keepdims=True)
        move = jnp.logical_and(done == 0, c_s < topk)
        return jnp.where(move, below, hi_s), jnp.where(move, c_s + 1, c_s)

    hi_s, c_s = lax.fori_loop(0, MAX_EXTRACT, step_down, (hi, c_hi))
    stepped = jnp.logical_and(done == 0, c_s == topk)
    c_chk = count_ge(jnp.where(stepped, hi_s, lo))
    lo = jnp.where(stepped, hi_s, lo)
    c_lo = jnp.where(stepped, c_chk, c_lo)
    done = jnp.where(c_lo == topk, 1, done)
    _, thr, _, n_ge, _, _ = lax.while_loop(keep_bisecting(all_done), search_step,
                                           (it, lo, hi, c_lo, c_hi, done))

    @pl.when(jnp.max(n_ge) > topk)
    def _():
        need = topk - count_where(lambda x, j: x > thr)

        idx_bits = (sc_ref.shape[0] * tk - 1).bit_length()

        def idx_step(t, c):
            cand = c | jnp.left_shift(jnp.int32(1), idx_bits - 1 - t)
            below = count_where(lambda x, j: jnp.logical_and(x == thr, j * tk + key_row < cand))
            return jnp.where(below < need, cand, c)

        cut = lax.fori_loop(0, idx_bits, idx_step, jnp.zeros((1, tq), I32))

        def demote_block(j, carry):
            x = sc_ref[j]
            drop = jnp.logical_and(x == thr, j * tk + key_row > cut)
            sc_ref[j] = jnp.where(drop, NEG_INF, x)
            return carry

        lax.fori_loop(0, nkb, demote_block, 0)

    slope_pieces = [_bf16_pieces(2.0 ** (-8.0 * (h + 1) / N_HEADS) * LOG2E) for h in range(N_HEADS)]
    slope_l2 = [sum(pieces) for pieces in slope_pieces]
    row_a = lax.broadcasted_iota(I32, (HEAD_DIM, tq), 0)
    for h in range(N_HEADS):
        hs = slice(h * HEAD_DIM, (h + 1) * HEAD_DIM)
        aug = jnp.zeros((HEAD_DIM, tq), F32)
        for r, c in enumerate(slope_pieces[h]):
            aug = jnp.where(row_a == 2 * r, c * POS_RADIX, jnp.where(row_a == 2 * r + 1, c, aug))
        qaug_ref[h, 0:HEAD_DIM, :] = qT_ref[hs, :]
        qaug_ref[h, HEAD_DIM:, :] = aug.astype(BF16)
    lane_p = lax.broadcasted_iota(I32, (tk, HEAD_DIM), 1)
    row_p = lax.broadcasted_iota(I32, (tk, HEAD_DIM), 0)
    pos = jnp.where(lane_p % 2 == 0, row_p // POS_RADIX, row_p % POS_RADIX)
    pos_ref[...] = jnp.where(lane_p < 2 * N_SLOPE_PIECES, pos, 0).astype(F32).astype(BF16)

    l_ref[...] = jnp.zeros(l_ref.shape, F32)
    acc_ref[...] = jnp.zeros(acc_ref.shape, F32)

    key_minus_query = key_row - (q_pos - i * tq)

    def logits_block(j, scores, m_run, slot):
        k0 = pl.multiple_of(j * tk, tk)
        k0f = jnp.asarray(j * tk).astype(F32)
        selected = jnp.logical_and(scores >= thr, key_minus_query <= i * tq - j * tk)
        bias = jnp.where(selected, 0.0, NEG_INF)
        m_blk = []
        for h in range(N_HEADS):
            hs = slice(h * HEAD_DIM, (h + 1) * HEAD_DIM)
            lhs = jnp.concatenate([k_ref[pl.ds(k0, tk), hs], pos_ref[...]], axis=1)
            s = _dot(lhs, qaug_ref[h]) + bias
            s_ref[slot, h] = s
            m_blk.append(jnp.max(s, axis=0, keepdims=True) + k0f * slope_l2[h])
        m_new = jnp.maximum(m_run, jnp.concatenate(m_blk, axis=0))
        return m_new, jnp.exp2(m_run - m_new)

    def values_block(j, m_j, alpha, slot):
        sums = []
        for h in range(N_HEADS):
            hs = slice(h * HEAD_DIM, (h + 1) * HEAD_DIM)
            m_local = m_j[h:h + 1, :] - jnp.asarray(j * tk).astype(F32) * slope_l2[h]
            p = jnp.exp2(s_ref[slot, h] - m_local)
            sums.append(jnp.sum(p, axis=0, keepdims=True))
            pv = _dot(vT_ref[j, hs, :], p.astype(BF16))
            acc_ref[hs, :] = alpha[h:h + 1, :] * acc_ref[hs, :] + pv
        l_ref[...] = alpha * l_ref[...] + jnp.concatenate(sums, axis=0)

    def attn_step(j, carry, slot, scores_next):
        m_j, alpha_j = carry
        nxt = logits_block(j + 1, scores_next, m_j, 1 - slot)
        values_block(j, m_j, alpha_j, slot)
        return nxt

    def attn_pair(p, carry):
        scores_a = sc_ref[2 * p + 1]
        scores_b = sc_ref[2 * p + 2]
        carry = attn_step(2 * p, carry, 0, scores_a)
        score_next(2 * p)
        carry = attn_step(2 * p + 1, carry, 1, scores_b)
        score_next(2 * p + 1)
        return carry

    gmax_ref[...] = jnp.full(gmax_ref.shape, -jnp.inf, F32)
    first = logits_block(0, sc_ref[0], jnp.full((N_HEADS, tq), -jnp.inf, F32), 0)
    n_pairs = (nkb - 1) // 2
    m_c, alpha_c = lax.fori_loop(0, n_pairs, attn_pair, first)
    j_rest = 2 * n_pairs

    @pl.when(j_rest == nkb - 1)
    def _():
        values_block(j_rest, m_c, alpha_c, 0)
        score_next(j_rest)

    @pl.when(j_rest < nkb - 1)
    def _():
        m_l, alpha_l = attn_step(j_rest, (m_c, alpha_c), 0, sc_ref[j_rest + 1])
        values_block(j_rest + 1, m_l, alpha_l, 1)
        score_next(j_rest)
        score_next(j_rest + 1)

    @pl.when(i + 1 < pl.num_programs(1))
    def _():
        score_block(nkb, qiTn_ref, wiTn_ref, (i + 1) * tq)

    inv_l = 1.0 / l_ref[...]
    for h in range(N_HEADS):
        hs = slice(h * HEAD_DIM, (h + 1) * HEAD_DIM)
        o_ref[:, hs] = (acc_ref[hs, :] * inv_l[h:h + 1, :]).T.astype(BF16)


def _dsa(qT, qiT, wiT, ki, k, vT, batch, seq, tq, tk):
    d_attn = qT.shape[1]
    nq = seq // tq
    topk = min(TOPK_MAX, seq // 4)
    assert tk >= topk and tk % COUNT_ROWS == 0 and seq % tk == 0 and tq == tk
    col = lambda b, i: (b, 0, i)
    col_next = lambda b, i: (b, 0, jnp.minimum(i + 1, nq - 1))
    kernel = functools.partial(_dsa_kernel, tq=tq, tk=tk, topk=topk)
    return pl.pallas_call(
        kernel,
        grid=(batch, nq),
        in_specs=[
            pl.BlockSpec((None, d_attn, tq), col),
            pl.BlockSpec((None, qiT.shape[1], tq), col),
            pl.BlockSpec((None, N_IDX_HEADS, tq), col),
            pl.BlockSpec((None, qiT.shape[1], tq), col_next),
            pl.BlockSpec((None, N_IDX_HEADS, tq), col_next),
            _resident((seq, IDX_DIM), lambda b, i: (b, 0)),
            _resident((seq, d_attn), lambda b, i: (b, 0)),
            _resident((None, seq // tk, d_attn, tk), lambda b, i: (b, 0, 0, 0)),
        ],
        out_specs=pl.BlockSpec((tq, d_attn), lambda b, i: (b * nq + i, 0)),
        out_shape=jax.ShapeDtypeStruct((batch * seq, d_attn), BF16),
        scratch_shapes=[
            pltpu.VMEM((seq // tk, tk, tq), F32),
            pltpu.VMEM((tk, tq), F32),
            pltpu.VMEM((N_HEADS, tq), F32),
            pltpu.VMEM((d_attn, tq), F32),
            pltpu.VMEM((N_HEADS, 2 * HEAD_DIM, tq), BF16),
            pltpu.VMEM((2, N_HEADS, tk, tq), F32),
            pltpu.VMEM((tk, HEAD_DIM), BF16),
        ],
        compiler_params=pltpu.CompilerParams(
            dimension_semantics=("arbitrary", "arbitrary"), vmem_limit_bytes=VMEM_LIMIT),
        name="dsa",
    )(qT, qiT, wiT, qiT, wiT, ki, k, vT)


def _merge_ln_kernel(h_ref, oa_ref, or_ref, ga_ref, gb_ref, wpa, wpr, wout, g_ref, b_ref, o_ref):
    ya = _dot(oa_ref[...], wpa[...])
    yr = _dot(or_ref[...], wpr[...])
    y = jax.nn.sigmoid(ga_ref[...]) * ya + jax.nn.sigmoid(gb_ref[...]) * yr
    mix = _dot(y.astype(BF16), wout[...])
    o_ref[...] = _layer_norm(ALPHA * h_ref[...] + mix, g_ref[...], b_ref[...])


def _merge_ln(h, o_attn, o_rnn, gab, wpa, wpr, wout, g, b, tm=FFN_ROWS):
    n, d = h.shape
    const = lambda i: (0, 0)
    return pl.pallas_call(
        _merge_ln_kernel,
        grid=(n // tm,),
        in_specs=[
            pl.BlockSpec((tm, d), lambda i: (i, 0)),
            pl.BlockSpec((tm, o_attn.shape[1]), lambda i: (i, 0)),
            pl.BlockSpec((tm, o_rnn.shape[1]), lambda i: (i, 0)),
            pl.BlockSpec((tm, d), lambda i: (i, 0)),
            pl.BlockSpec((tm, d), lambda i: (i, 1)),
            _resident(wpa.shape, const),
            _resident(wpr.shape, const),
            _resident(wout.shape, const),
            _resident((1, d), const),
            _resident((1, d), const),
        ],
        out_specs=pl.BlockSpec((tm, d), lambda i: (i, 0)),
        out_shape=jax.ShapeDtypeStruct((n, d), F32),
        compiler_params=pltpu.CompilerParams(
            dimension_semantics=("arbitrary",), vmem_limit_bytes=VMEM_LIMIT),
        name="merge_ln",
    )(h, o_attn, o_rnn, gab, gab, wpa, wpr, wout, g, b)


def _layer(x2, batch, seq, p, tq=DSA_BLOCK, tk=DSA_BLOCK):
    d_attn = N_HEADS * HEAD_DIM
    row = lambda a: a.reshape(1, -1).astype(F32)
    colv = lambda a: a.reshape(-1, 1).astype(F32)

    h1 = _ffn_ln(x2, p['ffn1_w_gate'].astype(BF16), p['ffn1_w_up'].astype(BF16),
                 p['ffn1_w_down'].astype(BF16), row(p['ln1_g']), row(p['ln1_b']))

    w_in, b_in = p['w_in'].astype(BF16), p['b_in']
    o = 0
    wq, bq = w_in[:, o:o + d_attn], b_in[o:o + d_attn]; o += d_attn
    wk, bk = w_in[:, o:o + d_attn], b_in[o:o + d_attn]; o += d_attn
    wv, bv = w_in[:, o:o + d_attn], b_in[o:o + d_attn]; o += d_attn
    n_qi = N_IDX_HEADS * IDX_DIM
    wqi, bqi = w_in[:, o:o + n_qi], b_in[o:o + n_qi]; o += n_qi
    wki, bki = w_in[:, o:o + IDX_DIM], b_in[o:o + IDX_DIM]; o += IDX_DIM
    wwi, bwi = w_in[:, o:o + N_IDX_HEADS], b_in[o:o + N_IDX_HEADS]; o += N_IDX_HEADS
    d_rnn = N_RNN_BLOCKS * RNN_BLOCK
    wxg, bxg = w_in[:, o:o + 2 * d_rnn], b_in[o:o + 2 * d_rnn]; o += 2 * d_rnn
    wab, bab = w_in[:, o:], b_in[o:]
    ws = [w.astype(BF16) for w in (wq.T, wk, wv.T, wqi.T, wki, wwi.T, wxg, wab)]
    bs = [colv(bq), row(bk), colv(bv), colv(bqi), row(bki), colv(bwi), row(bxg), row(bab)]
    w_gates = jnp.concatenate([p['lru_w_a'], p['lru_w_x']], axis=-1).astype(BF16)
    rnn = [p['conv_w'].astype(F32), row(p['conv_b']), w_gates, row(p['lru_b_a']), row(p['lru_b_x']),
           row(p['lru_lambda'])]
    qT, k, vT, qiT, ki, wiT, gab, o_rnn = _in_proj(h1, ws, bs, rnn, batch, seq, tk)

    o_attn = _dsa(qT, qiT, wiT, ki, k, vT, batch, seq, tq, tk)

    h2 = _merge_ln(h1, o_attn, o_rnn, gab, p['w_proj_attn'].astype(BF16),
                   p['w_proj_rnn'].astype(BF16), p['w_out'].astype(BF16),
                   row(p['ln2_g']), row(p['ln2_b']))

    return _ffn_ln(h2, p['ffn2_w_gate'].astype(BF16), p['ffn2_w_up'].astype(BF16),
                   p['ffn2_w_down'].astype(BF16), row(p['ln3_g']), row(p['ln3_b']))


_PARAM_NAMES = ('ln1_g', 'ln1_b', 'ffn1_w_gate', 'ffn1_w_up', 'ffn1_w_down', 'w_in', 'b_in', 'conv_w',
                'conv_b', 'lru_w_a', 'lru_b_a', 'lru_w_x', 'lru_b_x', 'lru_lambda', 'w_proj_attn',
                'w_proj_rnn', 'w_out', 'ln2_g', 'ln2_b', 'ffn2_w_gate', 'ffn2_w_up', 'ffn2_w_down',
                'ln3_g', 'ln3_b')


@jax.jit
def _forward(x, *params):
    batch, seq, d = x.shape
    h = x.reshape(batch * seq, d)
    for l in range(DEPTH):
        p = {name: a[l] for name, a in zip(_PARAM_NAMES, params)}
        h = _layer(h, batch, seq, p)
    return h.reshape(batch, seq, d)


def kernel(x, ln1_g, ln1_b, ffn1_w_gate, ffn1_w_up, ffn1_w_down, w_in, b_in, conv_w, conv_b, lru_w_a, lru_b_a, lru_w_x, lru_b_x, lru_lambda, w_proj_attn, w_proj_rnn, w_out, ln2_g, ln2_b, ffn2_w_gate, ffn2_w_up, ffn2_w_down, ln3_g, ln3_b):
    return _forward(x, ln1_g, ln1_b, ffn1_w_gate, ffn1_w_up, ffn1_w_down, w_in, b_in, conv_w, conv_b,
                    lru_w_a, lru_b_a, lru_w_x, lru_b_x, lru_lambda, w_proj_attn, w_proj_rnn, w_out,
                    ln2_g, ln2_b, ffn2_w_gate, ffn2_w_up, ffn2_w_down, ln3_g, ln3_b)
```
